```python
import math
import jax, jax.numpy as jnp
from jax import lax
import numpy as np

D_MODEL = 1024
BATCH = 16
SEQ = 2048
DEPTH = 4

GRID_W = 64
CTX_LEN = 256
HEAD_DIM = 64
GROUP_WIDTH = D_MODEL // 4
A_HEADS = GROUP_WIDTH // HEAD_DIM
A_QK_DIM = HEAD_DIM // 2
A_V_DIM = HEAD_DIM
B_HEADS = GROUP_WIDTH // HEAD_DIM
B_KEY_DIM = HEAD_DIM
B_VAL_DIM = HEAD_DIM
CHUNK = 64
C_Q_HEADS = GROUP_WIDTH // HEAD_DIM
C_KV_HEADS = C_Q_HEADS // 2
D_GROUPS = GROUP_WIDTH // HEAD_DIM
D_GROUP_DIM = HEAD_DIM
MIX_WIDTH = A_HEADS * A_V_DIM + B_HEADS * B_VAL_DIM + C_Q_HEADS * HEAD_DIM + D_GROUPS * D_GROUP_DIM
IN_SPLITS = (
    A_HEADS * 2 * A_QK_DIM, A_HEADS * 2 * A_QK_DIM, A_HEADS * A_V_DIM,
    B_HEADS * B_KEY_DIM, B_HEADS * B_KEY_DIM, B_HEADS * B_KEY_DIM,
    B_HEADS * B_VAL_DIM, B_HEADS * B_VAL_DIM,
    C_Q_HEADS * HEAD_DIM, C_KV_HEADS * HEAD_DIM, C_KV_HEADS * HEAD_DIM,
    D_GROUPS * D_GROUP_DIM,
)
IN_WIDTH = sum(IN_SPLITS)
FFN_HIDDEN = -(-8 * D_MODEL // (3 * 256)) * 256
Q_BLOCK = 128
ROPE_THETA = 10000.0
RMS_EPS = 1e-6

kernel_name = 'hybrid_parallel_heads_flow_block'

F32 = jnp.float32


def rmsnorm(x, gain):
    xf = x.astype(F32)
    y = xf * lax.rsqrt(jnp.mean(xf * xf, axis=-1, keepdims=True) + RMS_EPS)
    return (y * gain.astype(F32)).astype(x.dtype)


def modulate(h, shift, scale):
    return h * (1 + scale) + shift


def split_proj(p):
    offsets = np.cumsum(IN_SPLITS)[:-1].tolist()
    return jnp.split(p, offsets, axis=-1)


def axial_rope_tables(rows, dim):
    row = jnp.repeat(jnp.arange(rows), GRID_W).astype(F32)
    col = jnp.tile(jnp.arange(GRID_W), rows).astype(F32)
    n_freq = dim // 4
    inv_freq = ROPE_THETA ** (-jnp.arange(n_freq, dtype=F32) / n_freq)
    ang_r = row[:, None] * inv_freq[None, :]
    ang_c = col[:, None] * inv_freq[None, :]
    return (jnp.cos(ang_r), jnp.sin(ang_r), jnp.cos(ang_c), jnp.sin(ang_c))


def _rotate(x, cos, sin):
    n = x.shape[-1] // 2
    x1, x2 = x[..., :n], x[..., n:]
    cs, sn = cos[None, :, None, :], sin[None, :, None, :]
    return jnp.concatenate([x1 * cs - x2 * sn, x2 * cs + x1 * sn], axis=-1)


def apply_axial_rope(x, tabs):
    cos_r, sin_r, cos_c, sin_c = tabs
    xf = x.astype(F32)
    half = x.shape[-1] // 2
    out = jnp.concatenate([_rotate(xf[..., :half], cos_r, sin_r),
                           _rotate(xf[..., half:], cos_c, sin_c)], axis=-1)
    return out.astype(x.dtype)


def sweep_query_blocks(fn, qs):
    B, S = qs[0].shape[:2]
    nb = S // Q_BLOCK
    blocks = tuple(t.reshape(B, nb, Q_BLOCK, *t.shape[2:]).swapaxes(0, 1) for t in qs)
    out = lax.map(lambda bl: fn(*bl), blocks)
    return out.swapaxes(0, 1).reshape(B, S, *out.shape[3:])


def diff_attn_core(q1, q2, k1, k2, v, lam):
    scale = A_QK_DIM ** -0.5
    s1 = jnp.einsum('bqhd,bkhd->bhqk', q1.astype(F32), k1.astype(F32)) * scale
    s2 = jnp.einsum('bqhd,bkhd->bhqk', q2.astype(F32), k2.astype(F32)) * scale
    w = jax.nn.softmax(s1, axis=-1) - lam * jax.nn.softmax(s2, axis=-1)
    return jnp.einsum('bhqk,bkhd->bqhd', w, v.astype(F32)).astype(v.dtype)


def diff_attention_mixer(q, k, v, cq, ck, cv, lam_params, sub_gain, layer_idx, rope, need_ctx):
    B, S = q.shape[:2]
    Tc = cq.shape[1]

    def split_qk(t, pos):
        T = t.shape[1]
        t = t.reshape(B, T, A_HEADS * 2, A_QK_DIM)
        if pos is not None:
            t = apply_axial_rope(t, pos)
        t = t.reshape(B, T, A_HEADS, 2, A_QK_DIM)
        return t[..., 0, :], t[..., 1, :]

    q1, q2 = split_qk(q, rope)
    k1, k2 = split_qk(k, rope)
    cq1, cq2 = split_qk(cq, None)
    ck1, ck2 = split_qk(ck, None)
    vx = v.reshape(B, S, A_HEADS, A_V_DIM)
    vc = cv.reshape(B, Tc, A_HEADS, A_V_DIM)

    lam_init = 0.8 - 0.6 * math.exp(-0.3 * layer_idx)
    lp = lam_params.astype(F32)
    lam = jnp.exp(jnp.sum(lp[0] * lp[1])) - jnp.exp(jnp.sum(lp[2] * lp[3])) + lam_init

    k1_all = jnp.concatenate([ck1, k1], axis=1)
    k2_all = jnp.concatenate([ck2, k2], axis=1)
    v_all = jnp.concatenate([vc, vx], axis=1)
    o_x = sweep_query_blocks(lambda a, b: diff_attn_core(a, b, k1_all, k2_all, v_all, lam), (q1, q2))

    def finish(o):
        return (rmsnorm(o, sub_gain) * (1 - lam_init)).reshape(o.shape[0], o.shape[1], -1)

    out_c = finish(diff_attn_core(cq1, cq2, ck1, ck2, vc, lam)) if need_ctx else None
    return finish(o_x), out_c


def forget_gate(z, lb):
    sig = jax.nn.sigmoid(z)
    log_f = jnp.log(lb + (1 - lb) * sig)
    key = (1 - lb) * jax.nn.sigmoid(-z)
    return log_f, key


def gla_chunked(q, k, v, log_f, s0):
    B, T, H, _ = q.shape
    dv = v.shape[-1]
    n = T // CHUNK

    def to_chunks(t):
        return t.reshape(B, n, CHUNK, H, t.shape[-1]).transpose(1, 0, 3, 2, 4)

    incl = jnp.tril(jnp.ones((CHUNK, CHUNK), dtype=bool))[:, :, None]

    def step(state, inp):
        qc, kc, vc, gc = inp
        b = jnp.cumsum(gc, axis=2)
        o_inter = jnp.einsum('bhtd,bhde->bhte', qc * jnp.exp(b), state)
        rel = jnp.where(incl, b[:, :, :, None, :] - b[:, :, None, :, :], 0.0)
        decay = jnp.where(incl, jnp.exp(rel), 0.0)
        scores = jnp.einsum('bhtd,bhsd,bhtsd->bhts', qc, kc, decay)
        o_intra = jnp.einsum('bhts,bhse->bhte', scores, vc)
        b_end = b[:, :, -1:, :]
        new_state = (jnp.exp(b_end[:, :, 0, :])[..., None] * state
                     + jnp.einsum('bhsd,bhse->bhde', kc * jnp.exp(b_end - b), vc))
        return new_state, o_inter + o_intra

    s_end, o = lax.scan(step, s0, (to_chunks(q), to_chunks(k), to_chunks(v), to_chunks(log_f)))
    return o.transpose(1, 0, 3, 2, 4).reshape(B, T, H, dv), s_end


def gla_final_state(k, v, log_f):
    b = jnp.cumsum(log_f, axis=1)
    w = k * jnp.exp(b[:, -1:] - b)
    return jnp.einsum('bthd,bthe->bhde', w, v)


def hgrn2_mixer(q, f_fwd, f_bwd, i, g, cq, cf_fwd, cf_bwd, ci, cg, lb_fwd, lb_bwd, norm_gain, need_ctx):
    B, S = q.shape[:2]
    H, dk, dv = B_HEADS, B_KEY_DIM, B_VAL_DIM

    def heads(t, d):
        return t.reshape(t.shape[0], t.shape[1], H, d).astype(F32)

    def flip(t):
        return jnp.flip(t, axis=1)

    def ident(t):
        return t

    qx, ix, qc, ic = heads(q, dk), heads(i, dv), heads(cq, dk), heads(ci, dv)
    s0 = jnp.zeros((B, H, dk, dv), F32)
    o_x = jnp.zeros((B, S, H, dv), F32)
    o_c = jnp.zeros((B, cq.shape[1], H, dv), F32) if need_ctx else None
    for zx, zc, lb, order in ((f_fwd, cf_fwd, lb_fwd, ident), (f_bwd, cf_bwd, lb_bwd, flip)):
        lb = lb.reshape(H, dk)
        lfx, kx = forget_gate(heads(zx, dk), lb)
        lfc, kc = forget_gate(heads(zc, dk), lb)
        if need_ctx:
            oc, state = gla_chunked(order(qc), order(kc), order(ic), order(lfc), s0)
            o_c = o_c + order(oc)
        else:
            state = gla_final_state(order(kc), order(ic), order(lfc))
        ox, _ = gla_chunked(order(qx), order(kx), order(ix), order(lfx), state)
        o_x = o_x + order(ox)

    def finish(o, gate):
        T = o.shape[1]
        y = rmsnorm(o, norm_gain).reshape(B, T, H * dv) * jax.nn.silu(gate.astype(F32))
        return y.astype(gate.dtype)

    return finish(o_x, g), (finish(o_c, cg) if need_ctx else None)


def gqa_core(q, k, v):
    s = jnp.einsum('bqhgd,bkhd->bhgqk', q.astype(F32), k.astype(F32)) * HEAD_DIM ** -0.5
    p = jax.nn.softmax(s, axis=-1)
    return jnp.einsum('bhgqk,bkhd->bqhgd', p, v.astype(F32)).astype(v.dtype)


def gqa_mixer(q, k, v, cq, ck, cv, q_gain, k_gain, rope, need_ctx):
    B, S = q.shape[:2]
    G = C_Q_HEADS // C_KV_HEADS

    def prep(qt, kt, vt, pos):
        T = qt.shape[1]
        qh = rmsnorm(qt.reshape(B, T, C_Q_HEADS, HEAD_DIM), q_gain)
        kh = rmsnorm(kt.reshape(B, T, C_KV_HEADS, HEAD_DIM), k_gain)
        if pos is not None:
            qh = apply_axial_rope(qh, pos)
            kh = apply_axial_rope(kh, pos)
        return qh.reshape(B, T, C_KV_HEADS, G, HEAD_DIM), kh, vt.reshape(B, T, C_KV_HEADS, HEAD_DIM)

    qx, kx, vx = prep(q, k, v, rope)
    qc, kc, vc = prep(cq, ck, cv, None)
    k_all = jnp.concatenate([kc, kx], axis=1)
    v_all = jnp.concatenate([vc, vx], axis=1)
    out_x = sweep_query_blocks(lambda qb: gqa_core(qb, k_all, v_all), (qx,)).reshape(B, S, -1)
    out_c = gqa_core(qc, kc, vc).reshape(B, cq.shape[1], -1) if need_ctx else None
    return out_x, out_c


def fourier_mixer(u):
    B, T = u.shape[:2]
    grp = u.reshape(B, T, D_GROUPS, D_GROUP_DIM).astype(F32)
    y = jnp.fft.fftn(grp, axes=(1, 3), norm='ortho').real
    return y.reshape(B, T, D_GROUPS * D_GROUP_DIM).astype(u.dtype)


def swiglu(h, w_ffn_in, w_ffn_out):
    gate, up = jnp.split(h @ w_ffn_in, 2, axis=-1)
    return (jax.nn.silu(gate) * up) @ w_ffn_out


def hybrid_layer(x, ctx, c, c_ctx, w_mod, b_mod, norm1, w_in, diff_lambda, diff_norm,
                 lb_fwd, lb_bwd, hgrn_norm, q_norm, k_norm, w_out, norm2, w_ffn_in, w_ffn_out,
                 layer_idx, rope_a, rope_c, need_ctx):
    mod = jax.nn.silu(c) @ w_mod + b_mod
    mod_c = jax.nn.silu(c_ctx) @ w_mod + b_mod
    sh1, sc1, g1, sh2, sc2, g2 = jnp.split(mod[:, None, :], 6, axis=-1)
    csh1, csc1, cg1, csh2, csc2, cg2 = jnp.split(mod_c, 6, axis=-1)

    hx = modulate(rmsnorm(x, norm1), sh1, sc1)
    hc = modulate(rmsnorm(ctx, norm1), csh1, csc1)
    (xa_q, xa_k, xa_v, xb_q, xb_ff, xb_fb, xb_i, xb_g, xc_q, xc_k, xc_v, xd_u) = split_proj(hx @ w_in)
    (ca_q, ca_k, ca_v, cb_q, cb_ff, cb_fb, cb_i, cb_g, cc_q, cc_k, cc_v, cd_u) = split_proj(hc @ w_in)

    mA_x, mA_c = diff_attention_mixer(xa_q, xa_k, xa_v, ca_q, ca_k, ca_v, diff_lambda, diff_norm,
                                      layer_idx, rope_a, need_ctx)
    mB_x, mB_c = hgrn2_mixer(xb_q, xb_ff, xb_fb, xb_i, xb_g, cb_q, cb_ff, cb_fb, cb_i, cb_g,
                             lb_fwd, lb_bwd, hgrn_norm, need_ctx)
    mC_x, mC_c = gqa_mixer(xc_q, xc_k, xc_v, cc_q, cc_k, cc_v, q_norm, k_norm, rope_c, need_ctx)
    mD_x = fourier_mixer(xd_u)

    mix_x = jnp.concatenate([mA_x, mB_x, mC_x, mD_x], axis=-1)
    x = x + g1 * (mix_x @ w_out)
    x = x + g2 * swiglu(modulate(rmsnorm(x, norm2), sh2, sc2), w_ffn_in, w_ffn_out)

    if need_ctx:
        mix_c = jnp.concatenate([mA_c, mB_c, mC_c, fourier_mixer(cd_u)], axis=-1)
        ctx = ctx + cg1 * (mix_c @ w_out)
        ctx = ctx + cg2 * swiglu(modulate(rmsnorm(ctx, norm2), csh2, csc2), w_ffn_in, w_ffn_out)
    return x, ctx


def setup_inputs(seed: int = 0) -> dict:
    key = jax.random.key(seed)
    ks = jax.random.split(key, 19)
    D = D_MODEL
    nrm = jax.random.normal
    return {
        'x': nrm(ks[0], (BATCH, SEQ, D), F32),
        'c': nrm(ks[1], (BATCH, D), F32),
        'ctx': nrm(ks[2], (BATCH, CTX_LEN, D), F32),
        'c_ctx': nrm(ks[3], (D,), F32),
        'w_mod': nrm(ks[4], (DEPTH, D, 6 * D), F32) * (0.5 * D ** -0.5),
        'b_mod': nrm(ks[5], (DEPTH, 6 * D), F32) * 0.02,
        'norm1': 1.0 + 0.05 * nrm(ks[6], (DEPTH, D), F32),
        'w_in': nrm(ks[7], (DEPTH, D, IN_WIDTH), F32) * D ** -0.5,
        'diff_lambda': nrm(ks[8], (DEPTH, 4, A_QK_DIM), F32) * 0.1,
        'diff_norm': 1.0 + 0.05 * nrm(ks[9], (DEPTH, A_V_DIM), F32),
        'hgrn_lb_logits': nrm(ks[10], (2, DEPTH, B_HEADS * B_KEY_DIM), F32) * 0.5,
        'hgrn_norm': 1.0 + 0.05 * nrm(ks[11], (DEPTH, B_VAL_DIM), F32),
        'q_norm': 1.0 + 0.05 * nrm(ks[12], (DEPTH, HEAD_DIM), F32),
        'k_norm': 1.0 + 0.05 * nrm(ks[13], (DEPTH, HEAD_DIM), F32),
        'w_out': nrm(ks[14], (DEPTH, MIX_WIDTH, D), F32) * MIX_WIDTH ** -0.5,
        'norm2': 1.0 + 0.05 * nrm(ks[15], (DEPTH, D), F32),
        'w_ffn_in': nrm(ks[16], (DEPTH, D, 2 * FFN_HIDDEN), F32) * D ** -0.5,
        'w_ffn_out': nrm(ks[17], (DEPTH, FFN_HIDDEN, D), F32) * FFN_HIDDEN ** -0.5,
        'final_norm': 1.0 + 0.05 * nrm(ks[18], (D,), F32),
    }


def reference(x, c, ctx, c_ctx, w_mod, b_mod, norm1, w_in, diff_lambda, diff_norm,
              hgrn_lb_logits, hgrn_norm, q_norm, k_norm, w_out, norm2, w_ffn_in, w_ffn_out,
              final_norm):
    rows = x.shape[1] // GRID_W
    rope_a = axial_rope_tables(rows, A_QK_DIM)
    rope_c = axial_rope_tables(rows, HEAD_DIM)
    p = jax.nn.softmax(hgrn_lb_logits.astype(F32), axis=1)
    lower_bounds = jnp.cumsum(p, axis=1) - p[:, :1]
    for l in range(DEPTH):
        x, ctx = hybrid_layer(x, ctx, c, c_ctx, w_mod[l], b_mod[l], norm1[l], w_in[l],
                              diff_lambda[l], diff_norm[l], lower_bounds[0, l], lower_bounds[1, l],
                              hgrn_norm[l], q_norm[l], k_norm[l], w_out[l], norm2[l],
                              w_ffn_in[l], w_ffn_out[l], l, rope_a, rope_c, l < DEPTH - 1)
    return rmsnorm(x, final_norm)
```

```python
import functools
import math

import numpy as np
import jax
import jax.numpy as jnp
from jax import lax
from jax.experimental import pallas as pl
from jax.experimental.pallas import tpu as pltpu

F32 = jnp.float32
BF16 = jnp.bfloat16

HEAD_DIM = 64
GRID_W = 64
ROPE_THETA = 10000.0
RMS_EPS = 1e-6
GROUP = 256
A_QK = 32
TILE = 256
CHUNK = 64
LEVELS = 6
LOG2E = 1.4426950408889634
VMEM_LIMIT = 56 * 1024 * 1024

COL_A_Q, COL_A_K, COL_A_V = 0, 256, 512
COL_B_Q, COL_B_FF, COL_B_FB, COL_B_I, COL_B_G = 768, 1024, 1280, 1536, 1792
COL_C_Q, COL_C_K, COL_C_V = 2048, 2304, 2432
COL_D_U = 2560


def _cparams(n_axes):
    return pltpu.CompilerParams(dimension_semantics=("arbitrary",) * n_axes,
                                vmem_limit_bytes=VMEM_LIMIT)


def _dot(a, b):
    return jnp.dot(a, b, preferred_element_type=F32)


def _dot_nt(a, b):
    return lax.dot_general(a, b, (((1,), (1,)), ((), ())), preferred_element_type=F32)


def _dot_tn(a, b):
    return lax.dot_general(a, b, (((0,), (0,)), ((), ())), preferred_element_type=F32)


def _group_mean_sq(y, bd):
    y2 = y * y
    hi = y2.astype(BF16)
    lo = (y2 - hi.astype(F32)).astype(BF16)
    return _dot(hi, bd) + _dot(lo, bd)


def _rope(x, cos, sin_a, sin_b, shift):
    w = x.shape[-1]
    return x * cos + pltpu.roll(x, shift, 1) * sin_a + pltpu.roll(x, w - shift, 1) * sin_b


def _mod_kernel(c_ref, w_ref, b_ref, o_ref):
    c = c_ref[...]
    a = c * jax.nn.sigmoid(c)
    o_ref[0] = _dot(a.astype(BF16), w_ref[0].astype(BF16)) + b_ref[0]


def _modulation(cc, w_mod, b_mod):
    depth, d, n = w_mod.shape
    r = cc.shape[0]
    tn = 1536
    return pl.pallas_call(
        _mod_kernel,
        grid=(depth, n // tn),
        in_specs=[pl.BlockSpec((r, d), lambda l, j: (0, 0)),
                  pl.BlockSpec((1, d, tn), lambda l, j: (l, 0, j)),
                  pl.BlockSpec((1, 1, tn), lambda l, j: (l, 0, j))],
        out_specs=pl.BlockSpec((1, r, tn), lambda l, j: (l, 0, j)),
        out_shape=jax.ShapeDtypeStruct((depth, r, n), F32),
        compiler_params=_cparams(2),
        name="modulation",
    )(cc, w_mod, b_mod.reshape(depth, 1, n))


def _inproj_kernel(x_ref, mod_ref, n1_ref, w_ref, o_ref):
    x = x_ref[0]
    ms = jnp.mean(x * x, axis=-1, keepdims=True)
    y = x * lax.rsqrt(ms + RMS_EPS) * n1_ref[...]
    m = mod_ref[0, 0]
    h = y * (1.0 + m[1:2]) + m[0:1]
    o_ref[0] = _dot(h.astype(BF16), w_ref[...])


def _in_proj(xs, modsel, norm1, w_in):
    b, t, d = xs.shape
    n = w_in.shape[1]
    return pl.pallas_call(
        _inproj_kernel,
        grid=(b, t // TILE),
        in_specs=[pl.BlockSpec((1, TILE, d), lambda i, j: (i, j, 0)),
                  pl.BlockSpec((1, 1, 6, d), lambda i, j: (i, jnp.minimum(j, 1), 0, 0)),
                  pl.BlockSpec((1, d), lambda i, j: (0, 0)),
                  pl.BlockSpec((d, n), lambda i, j: (0, 0))],
        out_specs=pl.BlockSpec((1, TILE, n), lambda i, j: (i, j, 0)),
        out_shape=jax.ShapeDtypeStruct((b, t, n), F32),
        compiler_params=_cparams(2),
        name="in_proj",
    )(xs, modsel, norm1.reshape(1, d), w_in)


def _softmax_pv(qm, kk, vv):
    s = _dot_nt(qm, kk)
    m = jnp.max(s, axis=-1, keepdims=True)
    p = jnp.exp2(s - m)
    l = jnp.sum(p, axis=-1, keepdims=True)
    return _dot(p.astype(BF16), vv) * (1.0 / l)


def _attn_a_kernel(q_ref, k_ref, v_ref, cq_ref, saq_ref, sbq_ref, ck_ref, sak_ref, sbk_ref,
                   lp_ref, gain_ref, bd_ref, o_ref, kr_ref, vb_ref, *, tc, lam_init):
    t = pl.program_id(1)
    n_tok = kr_ref.shape[0]

    @pl.when(t == 0)
    def _():
        kr = _rope(k_ref[0], ck_ref[...], sak_ref[...], sbk_ref[...], A_QK // 4)
        kr_ref[...] = kr.astype(BF16)
        vb_ref[...] = v_ref[0].astype(BF16)

    qr = _rope(q_ref[0], cq_ref[...], saq_ref[...], sbq_ref[...], A_QK // 4)
    qr = qr * (A_QK ** -0.5 * LOG2E)
    lp = lp_ref[...]
    lam = (jnp.exp(jnp.sum(lp[0:1] * lp[1:2], axis=-1, keepdims=True))
           - jnp.exp(jnp.sum(lp[2:3] * lp[3:4], axis=-1, keepdims=True)) + lam_init)
    lane = lax.broadcasted_iota(jnp.int32, (1, GROUP), 1)

    def attend(nk):
        kk = kr_ref[0:nk, :]
        vv = vb_ref[0:nk, :]
        acc = jnp.zeros((TILE, GROUP), F32)
        for h in range(GROUP // HEAD_DIM):
            o1 = _softmax_pv(jnp.where(lane // A_QK == 2 * h, qr, 0.0).astype(BF16), kk, vv)
            o2 = _softmax_pv(jnp.where(lane // A_QK == 2 * h + 1, qr, 0.0).astype(BF16), kk, vv)
            acc = jnp.where(lane // HEAD_DIM == h, o1 - lam * o2, acc)
        y = acc * lax.rsqrt(_group_mean_sq(acc, bd_ref[...]) + RMS_EPS)
        o_ref[0] = y * gain_ref[...] * (1.0 - lam_init)

    @pl.when(t == 0)
    def _():
        attend(tc)

    @pl.when(t > 0)
    def _():
        attend(n_tok)


def _attn_a(proj, rope, lam_params, sub_gain, bd, tc, lam_init):
    b, t, _ = proj.shape
    cos, sa, sb = rope
    gain = jnp.tile(sub_gain, GROUP // HEAD_DIM).reshape(1, GROUP)
    tile_spec = pl.BlockSpec((TILE, GROUP), lambda i, j: (j, 0))
    full_spec = pl.BlockSpec((t, GROUP), lambda i, j: (0, 0))
    kern = functools.partial(_attn_a_kernel, tc=tc, lam_init=lam_init)
    return pl.pallas_call(
        kern,
        grid=(b, t // TILE),
        in_specs=[pl.BlockSpec((1, TILE, GROUP), lambda i, j: (i, j, COL_A_Q // GROUP)),
                  pl.BlockSpec((1, t, GROUP), lambda i, j: (i, 0, COL_A_K // GROUP)),
                  pl.BlockSpec((1, t, GROUP), lambda i, j: (i, 0, COL_A_V // GROUP)),
                  tile_spec, tile_spec, tile_spec, full_spec, full_spec, full_spec,
                  pl.BlockSpec(lam_params.shape, lambda i, j: (0, 0)),
                  pl.BlockSpec((1, GROUP), lambda i, j: (0, 0)),
                  pl.BlockSpec((GROUP, GROUP), lambda i, j: (0, 0))],
        out_specs=pl.BlockSpec((1, TILE, GROUP), lambda i, j: (i, j, 0)),
        out_shape=jax.ShapeDtypeStruct((b, t, GROUP), F32),
        scratch_shapes=[pltpu.VMEM((t, GROUP), BF16), pltpu.VMEM((t, GROUP), BF16)],
        compiler_params=_cparams(2),
        name="diff_attention",
    )(proj, proj, proj, cos, sa, sb, cos, sa, sb, lam_params, gain, bd)


def _attn_c_kernel(q_ref, k_ref, v_ref, cq_ref, saq_ref, sbq_ref, ck_ref, sak_ref, sbk_ref,
                   qg_ref, kg_ref, bd_ref, o_ref, kr_ref, vb_ref, *, tc):
    t = pl.program_id(1)
    n_tok = kr_ref.shape[0]
    half = GROUP // 2
    bd = bd_ref[...]

    @pl.when(t == 0)
    def _():
        k = k_ref[0]
        kn = k * lax.rsqrt(_group_mean_sq(k, bd[0:half, 0:half]) + RMS_EPS) * kg_ref[...]
        kr_ref[...] = _rope(kn, ck_ref[...], sak_ref[...], sbk_ref[...], HEAD_DIM // 4).astype(BF16)
        vb_ref[...] = v_ref[0].astype(BF16)

    q = q_ref[0]
    qn = q * lax.rsqrt(_group_mean_sq(q, bd) + RMS_EPS) * qg_ref[...]
    qr = _rope(qn, cq_ref[...], saq_ref[...], sbq_ref[...], HEAD_DIM // 4) * (HEAD_DIM ** -0.5 * LOG2E)
    lane = lax.broadcasted_iota(jnp.int32, (1, half), 1)

    def attend(nk):
        kk = kr_ref[0:nk, :]
        vv = vb_ref[0:nk, :]
        for qb in range(2):
            q128 = qr[:, qb * half:(qb + 1) * half]
            acc = jnp.zeros((TILE, half), F32)
            for g in range(2):
                o = _softmax_pv(jnp.where(lane // HEAD_DIM == g, q128, 0.0).astype(BF16), kk, vv)
                acc = jnp.where(lane // HEAD_DIM == g, o, acc)
            o_ref[0, :, qb * half:(qb + 1) * half] = acc

    @pl.when(t == 0)
    def _():
        attend(tc)

    @pl.when(t > 0)
    def _():
        attend(n_tok)


def _attn_c(proj, rope, q_gain, k_gain, bd, tc):
    b, t, _ = proj.shape
    cos, sa, sb = rope
    half = GROUP // 2
    qg = jnp.tile(q_gain, GROUP // HEAD_DIM).reshape(1, GROUP)
    kg = jnp.tile(k_gain, half // HEAD_DIM).reshape(1, half)
    tile_spec = pl.BlockSpec((TILE, GROUP), lambda i, j: (j, 0))
    full_spec = pl.BlockSpec((t, half), lambda i, j: (0, 0))
    return pl.pallas_call(
        functools.partial(_attn_c_kernel, tc=tc),
        grid=(b, t // TILE),
        in_specs=[pl.BlockSpec((1, TILE, GROUP), lambda i, j: (i, j, COL_C_Q // GROUP)),
                  pl.BlockSpec((1, t, half), lambda i, j: (i, 0, COL_C_K // half)),
                  pl.BlockSpec((1, t, half), lambda i, j: (i, 0, COL_C_V // half)),
                  tile_spec, tile_spec, tile_spec, full_spec, full_spec, full_spec,
                  pl.BlockSpec((1, GROUP), lambda i, j: (0, 0)),
                  pl.BlockSpec((1, half), lambda i, j: (0, 0)),
                  pl.BlockSpec((GROUP, GROUP), lambda i, j: (0, 0))],
        out_specs=pl.BlockSpec((1, TILE, GROUP), lambda i, j: (i, j, 0)),
        out_shape=jax.ShapeDtypeStruct((b, t, GROUP), F32),
        scratch_shapes=[pltpu.VMEM((t, half), BF16), pltpu.VMEM((t, half), BF16)],
        compiler_params=_cparams(2),
        name="gqa_attention",
    )(proj, proj, proj, cos, sa, sb, cos, sa, sb, qg, kg, bd)


def _fourier_kernel(u_ref, c64_ref, dctx_ref, dx_ref, o_ref, uc_ref, ux_ref, *, tc):
    t = pl.program_id(1)
    s = ux_ref.shape[0] // 2

    @pl.when(t == 0)
    def _():
        w = _dot(u_ref[0, 0:tc, :].astype(BF16), c64_ref[...])
        uc_ref[0:tc, :] = w[:, 0:GROUP].astype(BF16)
        uc_ref[tc:2 * tc, :] = w[:, GROUP:2 * GROUP].astype(BF16)
        o_ref[0] = _dot(dctx_ref[...], uc_ref[...]) * (tc ** -0.5)

    @pl.when(t == 1)
    def _():
        w = _dot(u_ref[0, tc:tc + s, :].astype(BF16), c64_ref[...])
        ux_ref[0:s, :] = w[:, 0:GROUP].astype(BF16)
        ux_ref[s:2 * s, :] = w[:, GROUP:2 * GROUP].astype(BF16)

    @pl.when(t > 0)
    def _():
        o_ref[0] = _dot(dx_ref[...], ux_ref[...]) * (s ** -0.5)


def _fourier(proj, c64, dctx, dx, tc):
    b, t, _ = proj.shape
    s = t - tc
    return pl.pallas_call(
        functools.partial(_fourier_kernel, tc=tc),
        grid=(b, t // TILE),
        in_specs=[pl.BlockSpec((1, t, GROUP), lambda i, j: (i, 0, COL_D_U // GROUP)),
                  pl.BlockSpec((GROUP, 2 * GROUP), lambda i, j: (0, 0)),
                  pl.BlockSpec((tc, 2 * tc), lambda i, j: (0, 0)),
                  pl.BlockSpec((TILE, 2 * s), lambda i, j: (jnp.maximum(j - 1, 0), 0))],
        out_specs=pl.BlockSpec((1, TILE, GROUP), lambda i, j: (i, j, 0)),
        out_shape=jax.ShapeDtypeStruct((b, t, GROUP), F32),
        scratch_shapes=[pltpu.VMEM((2 * tc, GROUP), BF16), pltpu.VMEM((2 * s, GROUP), BF16)],
        compiler_params=_cparams(2),
        name="fourier",
    )(proj, c64, dctx, dx)


def _dft_tables(tc, s):
    def cs(n):
        k = jnp.arange(n, dtype=jnp.int32)
        ang = ((k[:, None] * k[None, :]) % n).astype(F32) * (2.0 * math.pi / n)
        return jnp.cos(ang), jnp.sin(ang)

    c64, s64 = cs(HEAD_DIM)
    eye = jnp.eye(GROUP // HEAD_DIM, dtype=F32)
    c64 = jnp.concatenate([jnp.kron(eye, c64), jnp.kron(eye, s64)], axis=1) * (HEAD_DIM ** -0.5)
    cc, sc = cs(tc)
    cx, sx = cs(s)
    return (c64.astype(BF16), jnp.concatenate([cc, -sc], axis=1).astype(BF16),
            jnp.concatenate([cx, -sx], axis=1).astype(BF16))


def _hgrn_tables():
    c = CHUNK
    idx = np.arange(c)
    tri = (idx[None, :] <= idx[:, None]).astype(np.float32)
    lvl = np.full((c, c), -1.0, np.float32)
    lvl[idx, idx] = 0.0
    rows = [tri]
    for l in range(1, LEVELS + 1):
        w = c >> (l - 1)
        blk, pos = idx // w, idx % w
        ref = blk * w + w // 2 - 1
        rows.append(tri[ref])
        m = (blk[:, None] == blk[None, :]) & (pos[:, None] >= w // 2) & (pos[None, :] < w // 2)
        lvl[m] = float(l)
    w_f = np.concatenate(rows, axis=0)
    w_b = w_f.reshape(LEVELS + 1, c, c)[:, ::-1, ::-1].reshape(-1, c)
    heads = GROUP // HEAD_DIM
    head_mask = np.kron(np.eye(heads, dtype=np.float32), np.ones((c, HEAD_DIM), np.float32))
    return (np.tile(w_f, (1, 3)), np.tile(w_b, (1, 3)),
            np.tile(lvl, (1, heads)), np.tile(lvl.T, (1, heads)), head_mask)


def _hgrn_chunk(q, z, v, lb, w3, lvl, hm, st_ref, end_row):
    c = CHUNK
    e = jnp.exp(-jnp.abs(z))
    r = 1.0 / (1.0 + e)
    sig_pos = jnp.where(z >= 0, r, e * r)
    sig_neg = jnp.where(z >= 0, e * r, r)
    g = jnp.log(lb + (1.0 - lb) * sig_pos)
    k = (1.0 - lb) * sig_neg
    g_hi = g.astype(BF16)
    r1 = g - g_hi.astype(F32)
    g_mid = r1.astype(BF16)
    g_lo = (r1 - g_mid.astype(F32)).astype(BF16)
    sums = _dot(w3, jnp.concatenate([g_hi, g_mid, g_lo], axis=0))
    b = sums[0:c]
    b_end = b[end_row:end_row + 1]

    def block_diag(x):
        return jnp.where(hm > 0.5, jnp.concatenate([x] * (GROUP // HEAD_DIM), axis=0), 0.0)

    a = jnp.where(lvl == 0.0, _dot_nt(q.astype(BF16), block_diag(k).astype(BF16)), 0.0)
    for l in range(1, LEVELS + 1):
        d = jnp.exp(-jnp.abs(b - sums[l * c:(l + 1) * c]))
        a_l = _dot_nt((q * d).astype(BF16), block_diag(k * d).astype(BF16))
        a = jnp.where(lvl == float(l), a_l, a)
    st = st_ref[...]
    o = _dot(a.astype(BF16), block_diag(v).astype(BF16))
    o = o + _dot_nt((q * jnp.exp(b)).astype(BF16), st.astype(BF16))
    upd = _dot_tn(v.astype(BF16), (k * jnp.exp(b_end - b)).astype(BF16))
    st_ref[...] = st * jnp.exp(b_end) + jnp.where(hm > 0.5, upd, 0.0)
    return o


def _hgrn_kernel(qf_ref, zf_ref, vf_ref, qb_ref, zb_ref, vb_ref, lbf_ref, lbb_ref,
                 w3f_ref, w3b_ref, lvf_ref, lvb_ref, hm_ref, of_ref, ob_ref, sf_ref, sb_ref):
    @pl.when(pl.program_id(1) == 0)
    def _():
        sf_ref[...] = jnp.zeros_like(sf_ref)
        sb_ref[...] = jnp.zeros_like(sb_ref)

    hm = hm_ref[...]
    n = TILE // CHUNK
    for ci in range(n):
        rows = slice(ci * CHUNK, (ci + 1) * CHUNK)
        of_ref[0, rows, :] = _hgrn_chunk(qf_ref[0, rows, :], zf_ref[0, rows, :], vf_ref[0, rows, :],
                                         lbf_ref[...], w3f_ref[...], lvf_ref[...], hm, sf_ref,
                                         CHUNK - 1)
        rows = slice((n - 1 - ci) * CHUNK, (n - ci) * CHUNK)
        ob_ref[0, rows, :] = _hgrn_chunk(qb_ref[0, rows, :], zb_ref[0, rows, :], vb_ref[0, rows, :],
                                         lbb_ref[...], w3b_ref[...], lvb_ref[...], hm, sb_ref, 0)


def _hgrn(proj, lb_f, lb_b, tables):
    b, t, _ = proj.shape
    nt = t // TILE
    w3f, w3b, lvf, lvb, hm = tables

    def fwd(col):
        return pl.BlockSpec((1, TILE, GROUP), lambda i, j: (i, j, col // GROUP))

    def bwd(col):
        return pl.BlockSpec((1, TILE, GROUP),
                            lambda i, j: (i, jnp.where(j == 0, 0, nt - j), col // GROUP))

    def const(a):
        return pl.BlockSpec(a.shape, lambda i, j: (0,) * a.ndim)

    out_f = pl.BlockSpec((1, TILE, GROUP), lambda i, j: (i, j, 0))
    out_b = pl.BlockSpec((1, TILE, GROUP), lambda i, j: (i, jnp.where(j == 0, 0, nt - j), 0))
    lb_f = lb_f.reshape(1, GROUP)
    lb_b = lb_b.reshape(1, GROUP)
    return pl.pallas_call(
        _hgrn_kernel,
        grid=(b, nt),
        in_specs=[fwd(COL_B_Q), fwd(COL_B_FF), fwd(COL_B_I), bwd(COL_B_Q), bwd(COL_B_FB), bwd(COL_B_I),
                  const(lb_f), const(lb_b), const(w3f), const(w3b), const(lvf), const(lvb), const(hm)],
        out_specs=[out_f, out_b],
        out_shape=[jax.ShapeDtypeStruct((b, t, GROUP), F32)] * 2,
        scratch_shapes=[pltpu.VMEM((GROUP, GROUP), F32), pltpu.VMEM((GROUP, GROUP), F32)],
        compiler_params=_cparams(2),
        name="hgrn2",
    )(proj, proj, proj, proj, proj, proj, lb_f, lb_b, w3f, w3b, lvf, lvb, hm)


def _outproj_kernel(x_ref, ma_ref, of_ref, ob_ref, g_ref, mc_ref, md_ref, mod_ref, hg_ref, n2_ref,
                    bd_ref, w_ref, xo_ref, h_ref):
    o = of_ref[0] + ob_ref[0]
    g = g_ref[0]
    mb = (o * lax.rsqrt(_group_mean_sq(o, bd_ref[...]) + RMS_EPS) * hg_ref[...]) * (g * jax.nn.sigmoid(g))
    mix = jnp.concatenate([ma_ref[0], mb, mc_ref[0], md_ref[0]], axis=-1).astype(BF16)
    m = mod_ref[0, 0]
    x = x_ref[0] + m[2:3] * _dot(mix, w_ref[...])
    xo_ref[0] = x
    ms = jnp.mean(x * x, axis=-1, keepdims=True)
    y = x * lax.rsqrt(ms + RMS_EPS) * n2_ref[...]
    h_ref[0] = (y * (1.0 + m[4:5]) + m[3:4]).astype(BF16)


def _out_proj(xs, ma, o_f, o_b, proj, mc, md, modsel, hgrn_gain, norm2, bd, w_out, t0):
    b, t, d = xs.shape
    nt = t // TILE - t0

    def rows(width, col=0):
        return pl.BlockSpec((1, TILE, width), lambda i, j: (i, j + t0, col))

    hg = jnp.tile(hgrn_gain, GROUP // HEAD_DIM).reshape(1, GROUP)
    out_rows = pl.BlockSpec((1, TILE, d), lambda i, j: (i, j, 0))
    return pl.pallas_call(
        _outproj_kernel,
        grid=(b, nt),
        in_specs=[rows(d), rows(GROUP), rows(GROUP), rows(GROUP), rows(GROUP, COL_B_G // GROUP),
                  rows(GROUP), rows(GROUP),
                  pl.BlockSpec((1, 1, 6, d), lambda i, j: (i, jnp.minimum(j + t0, 1), 0, 0)),
                  pl.BlockSpec((1, GROUP), lambda i, j: (0, 0)),
                  pl.BlockSpec((1, d), lambda i, j: (0, 0)),
                  pl.BlockSpec((GROUP, GROUP), lambda i, j: (0, 0)),
                  pl.BlockSpec(w_out.shape, lambda i, j: (0, 0))],
        out_specs=[out_rows, out_rows],
        out_shape=[jax.ShapeDtypeStruct((b, nt * TILE, d), F32),
                   jax.ShapeDtypeStruct((b, nt * TILE, d), BF16)],
        compiler_params=_cparams(2),
        name="out_proj",
    )(xs, ma, o_f, o_b, proj, mc, md, modsel, hg, norm2.reshape(1, d), bd, w_out)


def _ffn_kernel(x_ref, h_ref, mod_ref, wi_ref, wo_ref, fn_ref, o_ref, *, hidden, final):
    h = h_ref[0]
    step = 256
    acc = jnp.zeros(x_ref.shape[1:], F32)
    for c in range(hidden // step):
        gate = _dot(h, wi_ref[:, c * step:(c + 1) * step])
        up = _dot(h, wi_ref[:, hidden + c * step:hidden + (c + 1) * step])
        act = gate * jax.nn.sigmoid(gate) * up
        acc = acc + _dot(act.astype(BF16), wo_ref[c * step:(c + 1) * step, :])
    x = x_ref[0] + mod_ref[0, 0][5:6] * acc
    if final:
        ms = jnp.mean(x * x, axis=-1, keepdims=True)
        x = x * lax.rsqrt(ms + RMS_EPS) * fn_ref[...]
    o_ref[0] = x


def _ffn(xs, h, modsel, w_in, w_out, final_norm, t0, final):
    b, t, d = xs.shape
    hidden = w_out.shape[0]
    rows = pl.BlockSpec((1, TILE, d), lambda i, j: (i, j, 0))
    return pl.pallas_call(
        functools.partial(_ffn_kernel, hidden=hidden, final=final),
        grid=(b, t // TILE),
        in_specs=[rows, rows,
                  pl.BlockSpec((1, 1, 6, d), lambda i, j: (i, jnp.minimum(j + t0, 1), 0, 0)),
                  pl.BlockSpec(w_in.shape, lambda i, j: (0, 0)),
                  pl.BlockSpec(w_out.shape, lambda i, j: (0, 0)),
                  pl.BlockSpec((1, d), lambda i, j: (0, 0))],
        out_specs=rows,
        out_shape=jax.ShapeDtypeStruct((b, t, d), F32),
        compiler_params=_cparams(2),
        name="swiglu",
    )(xs, h, modsel, w_in, w_out, final_norm.reshape(1, d))


def _rope_tables(tc, s, dim, width):
    pos = jnp.arange(s)
    row = (pos // GRID_W).astype(F32)
    col = (pos % GRID_W).astype(F32)
    n_freq = dim // 4
    inv_freq = ROPE_THETA ** (-jnp.arange(n_freq, dtype=F32) / n_freq)
    lane = np.arange(width) % dim
    is_col = (lane // (dim // 2)) == 1
    second = ((lane % (dim // 2)) // n_freq) == 1
    freq = inv_freq[lane % n_freq]
    ang = jnp.where(is_col[None, :], col[:, None], row[:, None]) * freq[None, :]
    cos, sin = jnp.cos(ang), jnp.sin(ang)
    sin_a = jnp.where(second[None, :], sin, 0.0)
    sin_b = jnp.where(second[None, :], 0.0, -sin)
    ident = jnp.ones((tc, width), F32)
    zero = jnp.zeros((tc, width), F32)
    return (jnp.concatenate([ident, cos], axis=0), jnp.concatenate([zero, sin_a], axis=0),
            jnp.concatenate([zero, sin_b], axis=0))


def kernel(x, c, ctx, c_ctx, w_mod, b_mod, norm1, w_in, diff_lambda, diff_norm, hgrn_lb_logits,
           hgrn_norm, q_norm, k_norm, w_out, norm2, w_ffn_in, w_ffn_out, final_norm):
    b, s, d = x.shape
    tc = ctx.shape[1]
    depth = w_mod.shape[0]
    assert tc == TILE and s % TILE == 0 and s % GRID_W == 0 and d == 4 * GROUP

    rope_a = _rope_tables(tc, s, A_QK, GROUP)
    rope_c = _rope_tables(tc, s, HEAD_DIM, GROUP)
    dft = _dft_tables(tc, s)
    hgrn_tabs = _hgrn_tables()
    hgrn_tabs = tuple(jnp.asarray(a, BF16 if i < 2 else F32) for i, a in enumerate(hgrn_tabs))
    bd = jnp.asarray(np.kron(np.eye(GROUP // HEAD_DIM), np.full((HEAD_DIM, HEAD_DIM), 1.0 / HEAD_DIM)), BF16)

    p = jax.nn.softmax(hgrn_lb_logits.astype(F32), axis=1)
    lower = jnp.cumsum(p, axis=1) - p[:, :1]

    hp = np.arange(GROUP).reshape(4, HEAD_DIM)[[0, 2, 1, 3]].reshape(-1)
    w_in_b = w_in.astype(BF16)
    w_in_b = w_in_b.at[:, :, COL_C_Q:COL_C_Q + GROUP].set(w_in_b[:, :, COL_C_Q + hp])
    w_out_b = w_out.astype(BF16)
    w_out_b = w_out_b.at[:, 2 * GROUP:3 * GROUP, :].set(w_out_b[:, 2 * GROUP + hp, :])
    w_ffn_in_b = w_ffn_in.astype(BF16)
    w_ffn_out_b = w_ffn_out.astype(BF16)

    rows = -(-(b + 1) // 8) * 8
    cc = jnp.zeros((rows, d), F32).at[:b].set(c).at[b].set(c_ctx)
    mod = _modulation(cc, w_mod, b_mod).reshape(depth, rows, 6, d)
    modsel = jnp.stack([jnp.broadcast_to(mod[:, b:b + 1], (depth, b, 6, d)), mod[:, :b]], axis=2)

    xs = jnp.concatenate([ctx, x], axis=1)
    for l in range(depth):
        last = l == depth - 1
        t0 = 1 if last else 0
        lam_init = 0.8 - 0.6 * math.exp(-0.3 * l)
        proj = _in_proj(xs, modsel[l], norm1[l], w_in_b[l])
        m_a = _attn_a(proj, rope_a, diff_lambda[l], diff_norm[l], bd, tc, lam_init)
        o_f, o_b = _hgrn(proj, lower[0, l], lower[1, l], hgrn_tabs)
        m_c = _attn_c(proj, rope_c, q_norm[l], k_norm[l], bd, tc)
        m_d = _fourier(proj, *dft, tc)
        xs, h = _out_proj(xs, m_a, o_f, o_b, proj, m_c, m_d, modsel[l], hgrn_norm[l], norm2[l], bd,
                          w_out_b[l], t0)
        xs = _ffn(xs, h, modsel[l], w_ffn_in_b[l], w_ffn_out_b[l], final_norm, t0, last)
    return xs
```

```python
import functools
import math

import numpy as np
import jax
import jax.numpy as jnp
from jax import lax
from jax.experimental import pallas as pl
from jax.experimental.pallas import tpu as pltpu

F32 = jnp.float32
BF16 = jnp.bfloat16

HEAD_DIM = 64
GRID_W = 64
ROPE_THETA = 10000.0
RMS_EPS = 1e-6
GROUP = 256
A_QK = 32
TILE = 256
CHUNK = 64
LEVELS = 6
LOG2E = 1.4426950408889634
VMEM_LIMIT = 56 * 1024 * 1024

COL_A_Q, COL_A_K, COL_A_V = 0, 256, 512
COL_B_Q, COL_B_FF, COL_B_FB, COL_B_I, COL_B_G = 768, 1024, 1280, 1536, 1792
COL_C_Q, COL_C_K, COL_C_V = 2048, 2304, 2432
COL_D_U = 2560


def _cparams(n_axes):
    return pltpu.CompilerParams(dimension_semantics=("arbitrary",) * n_axes,
                                vmem_limit_bytes=VMEM_LIMIT)


def _dot(a, b):
    return jnp.dot(a, b, preferred_element_type=F32)


def _dot_nt(a, b):
    return lax.dot_general(a, b, (((1,), (1,)), ((), ())), preferred_element_type=F32)


def _dot_tn(a, b):
    return lax.dot_general(a, b, (((0,), (0,)), ((), ())), preferred_element_type=F32)


def _group_mean_sq(y, bd):
    y2 = y * y
    hi = y2.astype(BF16)
    lo = (y2 - hi.astype(F32)).astype(BF16)
    return _dot(hi, bd) + _dot(lo, bd)


def _rope(x, cos, sin_a, sin_b, shift):
    w = x.shape[-1]
    return x * cos + pltpu.roll(x, shift, 1) * sin_a + pltpu.roll(x, w - shift, 1) * sin_b


def _mod_kernel(c_ref, w_ref, b_ref, o_ref):
    c = c_ref[...]
    a = c * jax.nn.sigmoid(c)
    o_ref[0] = _dot(a.astype(BF16), w_ref[0].astype(BF16)) + b_ref[0]


def _modulation(cc, w_mod, b_mod):
    depth, d, n = w_mod.shape
    r = cc.shape[0]
    tn = 1536
    return pl.pallas_call(
        _mod_kernel,
        grid=(depth, n // tn),
        in_specs=[pl.BlockSpec((r, d), lambda l, j: (0, 0)),
                  pl.BlockSpec((1, d, tn), lambda l, j: (l, 0, j)),
                  pl.BlockSpec((1, 1, tn), lambda l, j: (l, 0, j))],
        out_specs=pl.BlockSpec((1, r, tn), lambda l, j: (l, 0, j)),
        out_shape=jax.ShapeDtypeStruct((depth, r, n), F32),
        compiler_params=_cparams(2),
        name="modulation",
    )(cc, w_mod, b_mod.reshape(depth, 1, n))


def _mod_rows(mod_ref, sub, n_sub, with_ctx):
    m = mod_ref[0]
    if not (with_ctx and sub == n_sub - 1):
        return m[1]
    return jnp.where(pl.program_id(1) == pl.num_programs(1) - 1, m[0], m[1])


def _inproj_kernel(x_ref, mod_ref, n1_ref, w_ref, o_ref):
    n_sub = x_ref.shape[1] // TILE
    hs = []
    for i in range(n_sub):
        x = x_ref[0, i * TILE:(i + 1) * TILE, :]
        ms = jnp.mean(x * x, axis=-1, keepdims=True)
        y = x * lax.rsqrt(ms + RMS_EPS) * n1_ref[...]
        m = _mod_rows(mod_ref, i, n_sub, True)
        hs.append((y * (1.0 + m[1:2]) + m[0:1]).astype(BF16))
    o_ref[0] = _dot(jnp.concatenate(hs, axis=0), w_ref[...])


def _row_tile(t):
    return next(tm for tm in (3 * TILE, 2 * TILE, TILE) if t % tm == 0)


def _resident(shape):
    return pl.BlockSpec(shape, lambda i, j: (0,) * len(shape), pipeline_mode=pl.Buffered(1))


def _in_proj(xs, modsel, norm1, w_in):
    b, t, d = xs.shape
    n = w_in.shape[1]
    tm = _row_tile(t)
    return pl.pallas_call(
        _inproj_kernel,
        grid=(b, t // tm),
        in_specs=[pl.BlockSpec((1, tm, d), lambda i, j: (i, j, 0)),
                  pl.BlockSpec((1, 2, 6, d), lambda i, j: (i, 0, 0, 0)),
                  _resident((1, d)), _resident((d, n))],
        out_specs=pl.BlockSpec((1, tm, n), lambda i, j: (i, j, 0)),
        out_shape=jax.ShapeDtypeStruct((b, t, n), F32),
        compiler_params=_cparams(2),
        name="in_proj",
    )(xs, modsel, norm1.reshape(1, d), w_in)


def _softmax_pv(qm, kk, vv):
    s = _dot_nt(qm, kk)
    m = jnp.max(s, axis=-1, keepdims=True)
    p = jnp.exp2(s - m)
    l = jnp.sum(p, axis=-1, keepdims=True)
    return _dot(p.astype(BF16), vv) * (1.0 / l)


def _attn_a_kernel(q_ref, k_ref, v_ref, cq_ref, saq_ref, sbq_ref, ck_ref, sak_ref, sbk_ref,
                   lp_ref, gain_ref, bd_ref, o_ref, kr_ref, vb_ref, *, tc, lam_init):
    t = pl.program_id(1)
    n_tok = kr_ref.shape[0]

    @pl.when(t == 0)
    def _():
        kr = _rope(k_ref[0], ck_ref[...], sak_ref[...], sbk_ref[...], A_QK // 4)
        kr_ref[...] = kr.astype(BF16)
        vb_ref[...] = v_ref[0].astype(BF16)

    qr = _rope(q_ref[0], cq_ref[...], saq_ref[...], sbq_ref[...], A_QK // 4)
    qr = qr * (A_QK ** -0.5 * LOG2E)
    lp = lp_ref[...]
    lam = (jnp.exp(jnp.sum(lp[0:1] * lp[1:2], axis=-1, keepdims=True))
           - jnp.exp(jnp.sum(lp[2:3] * lp[3:4], axis=-1, keepdims=True)) + lam_init)
    lane = lax.broadcasted_iota(jnp.int32, (1, GROUP), 1)

    def attend(lo, nk):
        kk = kr_ref[lo:lo + nk, :]
        vv = vb_ref[lo:lo + nk, :]
        acc = jnp.zeros((TILE, GROUP), F32)
        for h in range(GROUP // HEAD_DIM):
            o1 = _softmax_pv(jnp.where(lane // A_QK == 2 * h, qr, 0.0).astype(BF16), kk, vv)
            o2 = _softmax_pv(jnp.where(lane // A_QK == 2 * h + 1, qr, 0.0).astype(BF16), kk, vv)
            acc = jnp.where(lane // HEAD_DIM == h, o1 - lam * o2, acc)
        y = acc * lax.rsqrt(_group_mean_sq(acc, bd_ref[...]) + RMS_EPS)
        o_ref[0] = y * gain_ref[...] * (1.0 - lam_init)

    @pl.when(t == pl.num_programs(1) - 1)
    def _():
        attend(n_tok - tc, tc)

    @pl.when(t < pl.num_programs(1) - 1)
    def _():
        attend(0, n_tok)


def _attn_a(proj, rope, lam_params, sub_gain, bd, tc, lam_init):
    b, t, _ = proj.shape
    cos, sa, sb = rope
    gain = jnp.tile(sub_gain, GROUP // HEAD_DIM).reshape(1, GROUP)
    tile_spec = pl.BlockSpec((TILE, GROUP), lambda i, j: (j, 0))
    full_spec = pl.BlockSpec((t, GROUP), lambda i, j: (0, 0))
    kern = functools.partial(_attn_a_kernel, tc=tc, lam_init=lam_init)
    return pl.pallas_call(
        kern,
        grid=(b, t // TILE),
        in_specs=[pl.BlockSpec((1, TILE, GROUP), lambda i, j: (i, j, COL_A_Q // GROUP)),
                  pl.BlockSpec((1, t, GROUP), lambda i, j: (i, 0, COL_A_K // GROUP)),
                  pl.BlockSpec((1, t, GROUP), lambda i, j: (i, 0, COL_A_V // GROUP)),
                  tile_spec, tile_spec, tile_spec, full_spec, full_spec, full_spec,
                  pl.BlockSpec(lam_params.shape, lambda i, j: (0, 0)),
                  pl.BlockSpec((1, GROUP), lambda i, j: (0, 0)),
                  pl.BlockSpec((GROUP, GROUP), lambda i, j: (0, 0))],
        out_specs=pl.BlockSpec((1, TILE, GROUP), lambda i, j: (i, j, 0)),
        out_shape=jax.ShapeDtypeStruct((b, t, GROUP), F32),
        scratch_shapes=[pltpu.VMEM((t, GROUP), BF16), pltpu.VMEM((t, GROUP), BF16)],
        compiler_params=_cparams(2),
        name="diff_attention",
    )(proj, proj, proj, cos, sa, sb, cos, sa, sb, lam_params, gain, bd)


def _attn_c_kernel(q_ref, k_ref, v_ref, cq_ref, saq_ref, sbq_ref, ck_ref, sak_ref, sbk_ref,
                   qg_ref, kg_ref, bd_ref, o_ref, kr_ref, vb_ref, *, tc):
    t = pl.program_id(1)
    n_tok = kr_ref.shape[0]
    half = GROUP // 2
    bd = bd_ref[...]

    @pl.when(t == 0)
    def _():
        k = k_ref[0]
        kn = k * lax.rsqrt(_group_mean_sq(k, bd[0:half, 0:half]) + RMS_EPS) * kg_ref[...]
        kr_ref[...] = _rope(kn, ck_ref[...], sak_ref[...], sbk_ref[...], HEAD_DIM // 4).astype(BF16)
        vb_ref[...] = v_ref[0].astype(BF16)

    q = q_ref[0]
    qn = q * lax.rsqrt(_group_mean_sq(q, bd) + RMS_EPS) * qg_ref[...]
    qr = _rope(qn, cq_ref[...], saq_ref[...], sbq_ref[...], HEAD_DIM // 4) * (HEAD_DIM ** -0.5 * LOG2E)
    lane = lax.broadcasted_iota(jnp.int32, (1, half), 1)

    def attend(lo, nk):
        kk = kr_ref[lo:lo + nk, :]
        vv = vb_ref[lo:lo + nk, :]
        for qb in range(2):
            q128 = qr[:, qb * half:(qb + 1) * half]
            acc = jnp.zeros((TILE, half), F32)
            for g in range(2):
                o = _softmax_pv(jnp.where(lane // HEAD_DIM == g, q128, 0.0).astype(BF16), kk, vv)
                acc = jnp.where(lane // HEAD_DIM == g, o, acc)
            o_ref[0, :, qb * half:(qb + 1) * half] = acc

    @pl.when(t == pl.num_programs(1) - 1)
    def _():
        attend(n_tok - tc, tc)

    @pl.when(t < pl.num_programs(1) - 1)
    def _():
        attend(0, n_tok)


def _attn_c(proj, rope, q_gain, k_gain, bd, tc):
    b, t, _ = proj.shape
    cos, sa, sb = rope
    half = GROUP // 2
    qg = jnp.tile(q_gain, GROUP // HEAD_DIM).reshape(1, GROUP)
    kg = jnp.tile(k_gain, half // HEAD_DIM).reshape(1, half)
    tile_spec = pl.BlockSpec((TILE, GROUP), lambda i, j: (j, 0))
    full_spec = pl.BlockSpec((t, half), lambda i, j: (0, 0))
    return pl.pallas_call(
        functools.partial(_attn_c_kernel, tc=tc),
        grid=(b, t // TILE),
        in_specs=[pl.BlockSpec((1, TILE, GROUP), lambda i, j: (i, j, COL_C_Q // GROUP)),
                  pl.BlockSpec((1, t, half), lambda i, j: (i, 0, COL_C_K // half)),
                  pl.BlockSpec((1, t, half), lambda i, j: (i, 0, COL_C_V // half)),
                  tile_spec, tile_spec, tile_spec, full_spec, full_spec, full_spec,
                  pl.BlockSpec((1, GROUP), lambda i, j: (0, 0)),
                  pl.BlockSpec((1, half), lambda i, j: (0, 0)),
                  pl.BlockSpec((GROUP, GROUP), lambda i, j: (0, 0))],
        out_specs=pl.BlockSpec((1, TILE, GROUP), lambda i, j: (i, j, 0)),
        out_shape=jax.ShapeDtypeStruct((b, t, GROUP), F32),
        scratch_shapes=[pltpu.VMEM((t, half), BF16), pltpu.VMEM((t, half), BF16)],
        compiler_params=_cparams(2),
        name="gqa_attention",
    )(proj, proj, proj, cos, sa, sb, cos, sa, sb, qg, kg, bd)


def _fourier_kernel(u_ref, c64_ref, dctx_ref, dx_ref, o_ref, uc_ref, ux_ref, *, tc):
    t = pl.program_id(1)
    last = pl.num_programs(1) - 1
    s = ux_ref.shape[0] // 2

    @pl.when(t == 0)
    def _():
        w = _dot(u_ref[0, 0:s, :].astype(BF16), c64_ref[...])
        ux_ref[0:s, :] = w[:, 0:GROUP].astype(BF16)
        ux_ref[s:2 * s, :] = w[:, GROUP:2 * GROUP].astype(BF16)

    @pl.when(t < last)
    def _():
        o_ref[0] = _dot(dx_ref[...], ux_ref[...]) * (s ** -0.5)

    @pl.when(t == last)
    def _():
        w = _dot(u_ref[0, s:s + tc, :].astype(BF16), c64_ref[...])
        uc_ref[0:tc, :] = w[:, 0:GROUP].astype(BF16)
        uc_ref[tc:2 * tc, :] = w[:, GROUP:2 * GROUP].astype(BF16)
        o_ref[0] = _dot(dctx_ref[...], uc_ref[...]) * (tc ** -0.5)


def _fourier(proj, c64, dctx, dx, tc):
    b, t, _ = proj.shape
    s = t - tc
    return pl.pallas_call(
        functools.partial(_fourier_kernel, tc=tc),
        grid=(b, t // TILE),
        in_specs=[pl.BlockSpec((1, t, GROUP), lambda i, j: (i, 0, COL_D_U // GROUP)),
                  pl.BlockSpec((GROUP, 2 * GROUP), lambda i, j: (0, 0)),
                  pl.BlockSpec((tc, 2 * tc), lambda i, j: (0, 0)),
                  pl.BlockSpec((TILE, 2 * s), lambda i, j: (jnp.minimum(j, s // TILE - 1), 0))],
        out_specs=pl.BlockSpec((1, TILE, GROUP), lambda i, j: (i, j, 0)),
        out_shape=jax.ShapeDtypeStruct((b, t, GROUP), F32),
        scratch_shapes=[pltpu.VMEM((2 * tc, GROUP), BF16), pltpu.VMEM((2 * s, GROUP), BF16)],
        compiler_params=_cparams(2),
        name="fourier",
    )(proj, c64, dctx, dx)


def _dft_tables(tc, s):
    def cs(n):
        k = jnp.arange(n, dtype=jnp.int32)
        ang = ((k[:, None] * k[None, :]) % n).astype(F32) * (2.0 * math.pi / n)
        return jnp.cos(ang), jnp.sin(ang)

    c64, s64 = cs(HEAD_DIM)
    eye = jnp.eye(GROUP // HEAD_DIM, dtype=F32)
    c64 = jnp.concatenate([jnp.kron(eye, c64), jnp.kron(eye, s64)], axis=1) * (HEAD_DIM ** -0.5)
    cc, sc = cs(tc)
    cx, sx = cs(s)
    return (c64.astype(BF16), jnp.concatenate([cc, -sc], axis=1).astype(BF16),
            jnp.concatenate([cx, -sx], axis=1).astype(BF16))


def _hgrn_tables():
    c = CHUNK
    idx = np.arange(c)
    tri = (idx[None, :] <= idx[:, None]).astype(np.float32)
    lvl = np.full((c, c), -1.0, np.float32)
    lvl[idx, idx] = 0.0
    rows = [tri]
    for l in range(1, LEVELS + 1):
        w = c >> (l - 1)
        blk, pos = idx // w, idx % w
        ref = blk * w + w // 2 - 1
        rows.append(tri[ref])
        m = (blk[:, None] == blk[None, :]) & (pos[:, None] >= w // 2) & (pos[None, :] < w // 2)
        lvl[m] = float(l)
    w_f = np.concatenate(rows, axis=0)
    w_b = w_f.reshape(LEVELS + 1, c, c)[:, ::-1, ::-1].reshape(-1, c)
    heads = GROUP // HEAD_DIM
    head_mask = np.kron(np.eye(heads, dtype=np.float32), np.ones((c, HEAD_DIM), np.float32))
    return (np.tile(w_f, (1, 3)), np.tile(w_b, (1, 3)),
            np.tile(lvl, (1, heads)), np.tile(lvl.T, (1, heads)), head_mask)


def _hgrn_chunk(q, z, v, lb, w3, lvl, hm, st_ref, end_row):
    c = CHUNK
    e = jnp.exp(-jnp.abs(z))
    r = 1.0 / (1.0 + e)
    sig_pos = jnp.where(z >= 0, r, e * r)
    sig_neg = jnp.where(z >= 0, e * r, r)
    g = jnp.log(lb + (1.0 - lb) * sig_pos)
    k = (1.0 - lb) * sig_neg
    g_hi = g.astype(BF16)
    r1 = g - g_hi.astype(F32)
    g_mid = r1.astype(BF16)
    g_lo = (r1 - g_mid.astype(F32)).astype(BF16)
    sums = _dot(w3, jnp.concatenate([g_hi, g_mid, g_lo], axis=0))
    b = sums[0:c]
    b_end = b[end_row:end_row + 1]

    def block_diag(x):
        return jnp.where(hm > 0.5, jnp.concatenate([x] * (GROUP // HEAD_DIM), axis=0), 0.0)

    a = jnp.where(lvl == 0.0, _dot_nt(q.astype(BF16), block_diag(k).astype(BF16)), 0.0)
    for l in range(1, LEVELS + 1):
        d = jnp.exp(-jnp.abs(b - sums[l * c:(l + 1) * c]))
        a_l = _dot_nt((q * d).astype(BF16), block_diag(k * d).astype(BF16))
        a = jnp.where(lvl == float(l), a_l, a)
    st = st_ref[...]
    o = _dot(a.astype(BF16), block_diag(v).astype(BF16))
    o = o + _dot_nt((q * jnp.exp(b)).astype(BF16), st.astype(BF16))
    upd = _dot_tn(v.astype(BF16), (k * jnp.exp(b_end - b)).astype(BF16))
    st_ref[...] = st * jnp.exp(b_end) + jnp.where(hm > 0.5, upd, 0.0)
    return o


def _hgrn_kernel(qf_ref, zf_ref, vf_ref, qb_ref, zb_ref, vb_ref, lbf_ref, lbb_ref,
                 w3f_ref, w3b_ref, lvf_ref, lvb_ref, hm_ref, of_ref, ob_ref, sf_ref, sb_ref):
    @pl.when(pl.program_id(1) == 0)
    def _():
        sf_ref[...] = jnp.zeros_like(sf_ref)
        sb_ref[...] = jnp.zeros_like(sb_ref)

    hm = hm_ref[...]
    n = TILE // CHUNK
    for ci in range(n):
        rows = slice(ci * CHUNK, (ci + 1) * CHUNK)
        of_ref[0, rows, :] = _hgrn_chunk(qf_ref[0, rows, :], zf_ref[0, rows, :], vf_ref[0, rows, :],
                                         lbf_ref[...], w3f_ref[...], lvf_ref[...], hm, sf_ref,
                                         CHUNK - 1)
        rows = slice((n - 1 - ci) * CHUNK, (n - ci) * CHUNK)
        ob_ref[0, rows, :] = _hgrn_chunk(qb_ref[0, rows, :], zb_ref[0, rows, :], vb_ref[0, rows, :],
                                         lbb_ref[...], w3b_ref[...], lvb_ref[...], hm, sb_ref, 0)


def _hgrn(proj, lb_f, lb_b, tables):
    b, t, _ = proj.shape
    nt = t // TILE
    w3f, w3b, lvf, lvb, hm = tables

    def fwd(col):
        return pl.BlockSpec((1, TILE, GROUP),
                            lambda i, j: (i, jnp.where(j == 0, nt - 1, j - 1), col // GROUP))

    def bwd(col):
        return pl.BlockSpec((1, TILE, GROUP), lambda i, j: (i, nt - 1 - j, col // GROUP))

    def const(a):
        return pl.BlockSpec(a.shape, lambda i, j: (0,) * a.ndim)

    out_f = fwd(0)
    out_b = bwd(0)
    lb_f = lb_f.reshape(1, GROUP)
    lb_b = lb_b.reshape(1, GROUP)
    return pl.pallas_call(
        _hgrn_kernel,
        grid=(b, nt),
        in_specs=[fwd(COL_B_Q), fwd(COL_B_FF), fwd(COL_B_I), bwd(COL_B_Q), bwd(COL_B_FB), bwd(COL_B_I),
                  const(lb_f), const(lb_b), const(w3f), const(w3b), const(lvf), const(lvb), const(hm)],
        out_specs=[out_f, out_b],
        out_shape=[jax.ShapeDtypeStruct((b, t, GROUP), F32)] * 2,
        scratch_shapes=[pltpu.VMEM((GROUP, GROUP), F32), pltpu.VMEM((GROUP, GROUP), F32)],
        compiler_params=_cparams(2),
        name="hgrn2",
    )(proj, proj, proj, proj, proj, proj, lb_f, lb_b, w3f, w3b, lvf, lvb, hm)


def _outffn_kernel(x_ref, ma_ref, of_ref, ob_ref, g_ref, mc_ref, md_ref, mod_ref, hg_ref, n2_ref,
                   fn_ref, bd_ref, wo_ref, wi_ref, wd_ref, o_ref, *, hidden, final):
    n_sub = x_ref.shape[1] // TILE
    o = of_ref[0] + ob_ref[0]
    g = g_ref[0]
    mb = (o * lax.rsqrt(_group_mean_sq(o, bd_ref[...]) + RMS_EPS) * hg_ref[...]) * (g * jax.nn.sigmoid(g))
    mix = jnp.concatenate([ma_ref[0], mb, mc_ref[0], md_ref[0]], axis=-1).astype(BF16)
    y = _dot(mix, wo_ref[...])
    xs, hs, gates = [], [], []
    for i in range(n_sub):
        m = _mod_rows(mod_ref, i, n_sub, not final)
        x = x_ref[0, i * TILE:(i + 1) * TILE, :] + m[2:3] * y[i * TILE:(i + 1) * TILE]
        ms = jnp.mean(x * x, axis=-1, keepdims=True)
        h = x * lax.rsqrt(ms + RMS_EPS) * n2_ref[...]
        xs.append(x)
        hs.append((h * (1.0 + m[4:5]) + m[3:4]).astype(BF16))
        gates.append(m[5:6])
    h = jnp.concatenate(hs, axis=0)
    step = 256
    acc = jnp.zeros(x_ref.shape[1:], F32)
    for c in range(hidden // step):
        gate = _dot(h, wi_ref[:, c * step:(c + 1) * step])
        up = _dot(h, wi_ref[:, hidden + c * step:hidden + (c + 1) * step])
        act = gate * jax.nn.sigmoid(gate) * up
        acc = acc + _dot(act.astype(BF16), wd_ref[c * step:(c + 1) * step, :])
    for i in range(n_sub):
        x = xs[i] + gates[i] * acc[i * TILE:(i + 1) * TILE]
        if final:
            ms = jnp.mean(x * x, axis=-1, keepdims=True)
            x = x * lax.rsqrt(ms + RMS_EPS) * fn_ref[...]
        o_ref[0, i * TILE:(i + 1) * TILE, :] = x


def _out_ffn(xs, ma, o_f, o_b, proj, mc, md, modsel, hgrn_gain, norm2, final_norm, bd, w_out,
             w_ffn_in, w_ffn_out, n_rows, final):
    b, _, d = xs.shape
    hidden = w_ffn_out.shape[0]
    tm = _row_tile(n_rows) if not final else 2 * TILE

    def rows(width, col=0):
        return pl.BlockSpec((1, tm, width), lambda i, j: (i, j, col))

    hg = jnp.tile(hgrn_gain, GROUP // HEAD_DIM).reshape(1, GROUP)
    return pl.pallas_call(
        functools.partial(_outffn_kernel, hidden=hidden, final=final),
        grid=(b, n_rows // tm),
        in_specs=[rows(d), rows(GROUP), rows(GROUP), rows(GROUP), rows(GROUP, COL_B_G // GROUP),
                  rows(GROUP), rows(GROUP),
                  pl.BlockSpec((1, 2, 6, d), lambda i, j: (i, 0, 0, 0)),
                  _resident((1, GROUP)), _resident((1, d)), _resident((1, d)),
                  _resident((GROUP, GROUP)), _resident(w_out.shape), _resident(w_ffn_in.shape),
                  _resident(w_ffn_out.shape)],
        out_specs=rows(d),
        out_shape=jax.ShapeDtypeStruct((b, n_rows, d), F32),
        compiler_params=_cparams(2),
        name="out_ffn",
    )(xs, ma, o_f, o_b, proj, mc, md, modsel, hg, norm2.reshape(1, d), final_norm.reshape(1, d), bd,
      w_out, w_ffn_in, w_ffn_out)


def _rope_tables(tc, s, dim, width):
    pos = jnp.arange(s)
    row = (pos // GRID_W).astype(F32)
    col = (pos % GRID_W).astype(F32)
    n_freq = dim // 4
    inv_freq = ROPE_THETA ** (-jnp.arange(n_freq, dtype=F32) / n_freq)
    lane = np.arange(width) % dim
    is_col = (lane // (dim // 2)) == 1
    second = ((lane % (dim // 2)) // n_freq) == 1
    freq = inv_freq[lane % n_freq]
    ang = jnp.where(is_col[None, :], col[:, None], row[:, None]) * freq[None, :]
    cos, sin = jnp.cos(ang), jnp.sin(ang)
    sin_a = jnp.where(second[None, :], sin, 0.0)
    sin_b = jnp.where(second[None, :], 0.0, -sin)
    ident = jnp.ones((tc, width), F32)
    zero = jnp.zeros((tc, width), F32)
    return (jnp.concatenate([cos, ident], axis=0), jnp.concatenate([sin_a, zero], axis=0),
            jnp.concatenate([sin_b, zero], axis=0))


def kernel(x, c, ctx, c_ctx, w_mod, b_mod, norm1, w_in, diff_lambda, diff_norm, hgrn_lb_logits,
           hgrn_norm, q_norm, k_norm, w_out, norm2, w_ffn_in, w_ffn_out, final_norm):
    b, s, d = x.shape
    tc = ctx.shape[1]
    depth = w_mod.shape[0]
    assert tc == TILE and s % TILE == 0 and s % GRID_W == 0 and d == 4 * GROUP

    rope_a = _rope_tables(tc, s, A_QK, GROUP)
    rope_c = _rope_tables(tc, s, HEAD_DIM, GROUP)
    dft = _dft_tables(tc, s)
    hgrn_tabs = _hgrn_tables()
    hgrn_tabs = tuple(jnp.asarray(a, BF16 if i < 2 else F32) for i, a in enumerate(hgrn_tabs))
    bd = jnp.asarray(np.kron(np.eye(GROUP // HEAD_DIM), np.full((HEAD_DIM, HEAD_DIM), 1.0 / HEAD_DIM)), BF16)

    p = jax.nn.softmax(hgrn_lb_logits.astype(F32), axis=1)
    lower = jnp.cumsum(p, axis=1) - p[:, :1]

    hp = np.arange(GROUP).reshape(4, HEAD_DIM)[[0, 2, 1, 3]].reshape(-1)
    w_in_b = w_in.astype(BF16)
    w_in_b = w_in_b.at[:, :, COL_C_Q:COL_C_Q + GROUP].set(w_in_b[:, :, COL_C_Q + hp])
    w_out_b = w_out.astype(BF16)
    w_out_b = w_out_b.at[:, 2 * GROUP:3 * GROUP, :].set(w_out_b[:, 2 * GROUP + hp, :])
    w_ffn_in_b = w_ffn_in.astype(BF16)
    w_ffn_out_b = w_ffn_out.astype(BF16)

    rows = -(-(b + 1) // 8) * 8
    cc = jnp.zeros((rows, d), F32).at[:b].set(c).at[b].set(c_ctx)
    mod = _modulation(cc, w_mod, b_mod).reshape(depth, rows, 6, d)
    modsel = jnp.stack([jnp.broadcast_to(mod[:, b:b + 1], (depth, b, 6, d)), mod[:, :b]], axis=2)

    xs = jnp.concatenate([x, ctx], axis=1)
    for l in range(depth):
        last = l == depth - 1
        lam_init = 0.8 - 0.6 * math.exp(-0.3 * l)
        proj = _in_proj(xs, modsel[l], norm1[l], w_in_b[l])
        m_a = _attn_a(proj, rope_a, diff_lambda[l], diff_norm[l], bd, tc, lam_init)
        o_f, o_b = _hgrn(proj, lower[0, l], lower[1, l], hgrn_tabs)
        m_c = _attn_c(proj, rope_c, q_norm[l], k_norm[l], bd, tc)
        m_d = _fourier(proj, *dft, tc)
        xs = _out_ffn(xs, m_a, o_f, o_b, proj, m_c, m_d, modsel[l], hgrn_norm[l], norm2[l], final_norm,
                      bd, w_out_b[l], w_ffn_in_b[l], w_ffn_out_b[l], s if last else s + tc, last)
    return xs
```

```python
import functools
import math

import numpy as np
import jax
import jax.numpy as jnp
from jax import lax
from jax.experimental import pallas as pl
from jax.experimental.pallas import tpu as pltpu

F32 = jnp.float32
BF16 = jnp.bfloat16

HEAD_DIM = 64
GRID_W = 64
ROPE_THETA = 10000.0
RMS_EPS = 1e-6
GROUP = 256
A_QK = 32
TILE = 256
CHUNK = 64
LEVELS = 6
LOG2E = 1.4426950408889634
VMEM_LIMIT = 56 * 1024 * 1024

COL_A_Q, COL_A_K, COL_A_V = 0, 256, 512
COL_B_Q, COL_B_FF, COL_B_FB, COL_B_I, COL_B_G = 768, 1024, 1280, 1536, 1792
COL_C_Q, COL_C_K, COL_C_V = 2048, 2304, 2432
COL_D_U = 2560


def _cparams(n_axes):
    return pltpu.CompilerParams(dimension_semantics=("arbitrary",) * n_axes,
                                vmem_limit_bytes=VMEM_LIMIT)


def _dot(a, b):
    return jnp.dot(a, b, preferred_element_type=F32)


def _dot_nt(a, b):
    return lax.dot_general(a, b, (((1,), (1,)), ((), ())), preferred_element_type=F32)


def _dot_tn(a, b):
    return lax.dot_general(a, b, (((0,), (0,)), ((), ())), preferred_element_type=F32)


def _group_mean_sq(y, bd):
    y2 = y * y
    hi = y2.astype(BF16)
    lo = (y2 - hi.astype(F32)).astype(BF16)
    return _dot(hi, bd) + _dot(lo, bd)


def _rope(x, cos, sin_a, sin_b, shift):
    w = x.shape[-1]
    return x * cos + pltpu.roll(x, shift, 1) * sin_a + pltpu.roll(x, w - shift, 1) * sin_b


def _mod_kernel(c_ref, w_ref, b_ref, o_ref):
    c = c_ref[...]
    a = c * jax.nn.sigmoid(c)
    o_ref[0] = _dot(a.astype(BF16), w_ref[0].astype(BF16)) + b_ref[0]


def _modulation(cc, w_mod, b_mod):
    depth, d, n = w_mod.shape
    r = cc.shape[0]
    tn = 1536
    return pl.pallas_call(
        _mod_kernel,
        grid=(depth, n // tn),
        in_specs=[pl.BlockSpec((r, d), lambda l, j: (0, 0)),
                  pl.BlockSpec((1, d, tn), lambda l, j: (l, 0, j)),
                  pl.BlockSpec((1, 1, tn), lambda l, j: (l, 0, j))],
        out_specs=pl.BlockSpec((1, r, tn), lambda l, j: (l, 0, j)),
        out_shape=jax.ShapeDtypeStruct((depth, r, n), F32),
        compiler_params=_cparams(2),
        name="modulation",
    )(cc, w_mod, b_mod.reshape(depth, 1, n))


def _mod_rows(mod_ref, sub, n_sub, with_ctx):
    m = mod_ref[0]
    if not (with_ctx and sub == n_sub - 1):
        return m[1]
    return jnp.where(pl.program_id(1) == pl.num_programs(1) - 1, m[0], m[1])


def _inproj_kernel(x_ref, mod_ref, n1_ref, w_ref, o_ref):
    n_sub = x_ref.shape[1] // TILE
    hs = []
    for i in range(n_sub):
        x = x_ref[0, i * TILE:(i + 1) * TILE, :]
        ms = jnp.mean(x * x, axis=-1, keepdims=True)
        y = x * lax.rsqrt(ms + RMS_EPS) * n1_ref[...]
        m = _mod_rows(mod_ref, i, n_sub, True)
        hs.append((y * (1.0 + m[1:2]) + m[0:1]).astype(BF16))
    o_ref[0] = _dot(jnp.concatenate(hs, axis=0), w_ref[...])


def _row_tile(t):
    return next(tm for tm in (3 * TILE, 2 * TILE, TILE) if t % tm == 0)


def _resident(shape):
    return pl.BlockSpec(shape, lambda i, j: (0,) * len(shape), pipeline_mode=pl.Buffered(1))


def _in_proj(xs, modsel, norm1, w_in):
    b, t, d = xs.shape
    n = w_in.shape[1]
    tm = _row_tile(t)
    return pl.pallas_call(
        _inproj_kernel,
        grid=(b, t // tm),
        in_specs=[pl.BlockSpec((1, tm, d), lambda i, j: (i, j, 0)),
                  pl.BlockSpec((1, 2, 6, d), lambda i, j: (i, 0, 0, 0)),
                  _resident((1, d)), _resident((d, n))],
        out_specs=pl.BlockSpec((1, tm, n), lambda i, j: (i, j, 0)),
        out_shape=jax.ShapeDtypeStruct((b, t, n), F32),
        compiler_params=_cparams(2),
        name="in_proj",
    )(xs, modsel, norm1.reshape(1, d), w_in)


V_ROWS = HEAD_DIM + 16


KEY_TILE = 256


def _attend_t(kr_ref, vt_ref, s_ref, wqs, vt_index, lo, nk):
    n_kt = nk // KEY_TILE
    n_g = len(wqs)
    maxes = [None] * n_g
    accs = [None] * n_g

    def keys(kt):
        return slice(lo + kt * KEY_TILE, lo + (kt + 1) * KEY_TILE)

    def score_tile(g, kt):
        s = _dot_nt(kr_ref[keys(kt), :], wqs[g])
        s_ref[g % 2, kt * KEY_TILE:(kt + 1) * KEY_TILE, :] = s
        m = jnp.max(s, axis=0, keepdims=True)
        maxes[g] = m if kt == 0 else jnp.maximum(maxes[g], m)

    def value_tile(g, kt, p):
        pv = _dot(vt_ref[vt_index[g], :, keys(kt)], p)
        accs[g] = pv if kt == 0 else accs[g] + pv

    for kt in range(n_kt):
        score_tile(0, kt)
    for g in range(n_g):
        p_prev = None
        for kt in range(n_kt):
            if g + 1 < n_g:
                score_tile(g + 1, kt)
            p = jnp.exp2(s_ref[g % 2, kt * KEY_TILE:(kt + 1) * KEY_TILE, :] - maxes[g]).astype(BF16)
            if kt > 0:
                value_tile(g, kt - 1, p_prev)
            p_prev = p
        value_tile(g, n_kt - 1, p_prev)
    return [a[0:HEAD_DIM] * (1.0 / a[HEAD_DIM:HEAD_DIM + 1]) for a in accs]


def _store_values_t(vt_ref, v):
    v_t = v.T
    for h in range(vt_ref.shape[0]):
        vt_ref[h, 0:HEAD_DIM, :] = v_t[h * HEAD_DIM:(h + 1) * HEAD_DIM, :].astype(BF16)
        vt_ref[h, HEAD_DIM:V_ROWS, :] = jnp.ones((V_ROWS - HEAD_DIM, v.shape[0]), BF16)


def _attn_a_kernel(q_ref, k_ref, v_ref, cq_ref, saq_ref, sbq_ref, ck_ref, sak_ref, sbk_ref,
                   lp_ref, gain_ref, bd_ref, o_ref, kr_ref, vt_ref, s_ref, *, tc, lam_init):
    t = pl.program_id(1)
    n_tok = kr_ref.shape[0]

    @pl.when(t == 0)
    def _():
        kr = _rope(k_ref[0], ck_ref[...], sak_ref[...], sbk_ref[...], A_QK // 4)
        kr_ref[...] = kr.astype(BF16)
        _store_values_t(vt_ref, v_ref[0])

    qr = _rope(q_ref[0], cq_ref[...], saq_ref[...], sbq_ref[...], A_QK // 4)
    qr = qr * (A_QK ** -0.5 * LOG2E)
    lp = lp_ref[...]
    lam = (jnp.exp(jnp.sum(lp[0:1] * lp[1:2], axis=-1, keepdims=True))
           - jnp.exp(jnp.sum(lp[2:3] * lp[3:4], axis=-1, keepdims=True)) + lam_init)
    lane = lax.broadcasted_iota(jnp.int32, (1, GROUP), 1)

    def attend(lo, nk):
        n_heads = GROUP // HEAD_DIM
        wqs = [jnp.concatenate([jnp.where(lane // A_QK == 2 * h + c, qr, 0.0).astype(BF16)
                                for c in range(2)], axis=0) for h in range(n_heads)]
        outs = _attend_t(kr_ref, vt_ref, s_ref, wqs, list(range(n_heads)), lo, nk)
        heads = [o[:, 0:TILE] - lam * o[:, TILE:2 * TILE] for o in outs]
        acc = jnp.concatenate(heads, axis=0).T
        y = acc * lax.rsqrt(_group_mean_sq(acc, bd_ref[...]) + RMS_EPS)
        o_ref[0] = y * gain_ref[...] * (1.0 - lam_init)

    @pl.when(t == pl.num_programs(1) - 1)
    def _():
        attend(n_tok - tc, tc)

    @pl.when(t < pl.num_programs(1) - 1)
    def _():
        attend(0, n_tok)


def _attn_a(proj, rope, lam_params, sub_gain, bd, tc, lam_init):
    b, t, _ = proj.shape
    cos, sa, sb = rope
    gain = jnp.tile(sub_gain, GROUP // HEAD_DIM).reshape(1, GROUP)
    tile_spec = pl.BlockSpec((TILE, GROUP), lambda i, j: (j, 0))
    full_spec = pl.BlockSpec((t, GROUP), lambda i, j: (0, 0))
    kern = functools.partial(_attn_a_kernel, tc=tc, lam_init=lam_init)
    return pl.pallas_call(
        kern,
        grid=(b, t // TILE),
        in_specs=[pl.BlockSpec((1, TILE, GROUP), lambda i, j: (i, j, COL_A_Q // GROUP)),
                  pl.BlockSpec((1, t, GROUP), lambda i, j: (i, 0, COL_A_K // GROUP)),
                  pl.BlockSpec((1, t, GROUP), lambda i, j: (i, 0, COL_A_V // GROUP)),
                  tile_spec, tile_spec, tile_spec, full_spec, full_spec, full_spec,
                  pl.BlockSpec(lam_params.shape, lambda i, j: (0, 0)),
                  pl.BlockSpec((1, GROUP), lambda i, j: (0, 0)),
                  pl.BlockSpec((GROUP, GROUP), lambda i, j: (0, 0))],
        out_specs=pl.BlockSpec((1, TILE, GROUP), lambda i, j: (i, j, 0)),
        out_shape=jax.ShapeDtypeStruct((b, t, GROUP), F32),
        scratch_shapes=[pltpu.VMEM((t, GROUP), BF16),
                        pltpu.VMEM((GROUP // HEAD_DIM, V_ROWS, t), BF16),
                        pltpu.VMEM((2, t, 2 * TILE), F32)],
        compiler_params=_cparams(2),
        name="diff_attention",
    )(proj, proj, proj, cos, sa, sb, cos, sa, sb, lam_params, gain, bd)


def _attn_c_kernel(q_ref, k_ref, v_ref, cq_ref, saq_ref, sbq_ref, ck_ref, sak_ref, sbk_ref,
                   qg_ref, kg_ref, bd_ref, o_ref, kr_ref, vt_ref, s_ref, *, tc):
    t = pl.program_id(1)
    n_tok = kr_ref.shape[0]
    half = GROUP // 2
    bd = bd_ref[...]

    @pl.when(t == 0)
    def _():
        k = k_ref[0]
        kn = k * lax.rsqrt(_group_mean_sq(k, bd[0:half, 0:half]) + RMS_EPS) * kg_ref[...]
        kr_ref[...] = _rope(kn, ck_ref[...], sak_ref[...], sbk_ref[...], HEAD_DIM // 4).astype(BF16)
        _store_values_t(vt_ref, v_ref[0])

    q = q_ref[0]
    qn = q * lax.rsqrt(_group_mean_sq(q, bd) + RMS_EPS) * qg_ref[...]
    qr = _rope(qn, cq_ref[...], saq_ref[...], sbq_ref[...], HEAD_DIM // 4) * (HEAD_DIM ** -0.5 * LOG2E)
    lane = lax.broadcasted_iota(jnp.int32, (1, half), 1)

    def attend(lo, nk):
        wqs = [jnp.concatenate([jnp.where(lane // HEAD_DIM == g, qr[:, qb * half:(qb + 1) * half], 0.0)
                                for qb in range(2)], axis=0).astype(BF16) for g in range(2)]
        outs = _attend_t(kr_ref, vt_ref, s_ref, wqs, [0, 1], lo, nk)
        heads = [outs[g][:, qb * TILE:(qb + 1) * TILE] for qb in range(2) for g in range(2)]
        o_ref[0] = jnp.concatenate(heads, axis=0).T

    @pl.when(t == pl.num_programs(1) - 1)
    def _():
        attend(n_tok - tc, tc)

    @pl.when(t < pl.num_programs(1) - 1)
    def _():
        attend(0, n_tok)


def _attn_c(proj, rope, q_gain, k_gain, bd, tc):
    b, t, _ = proj.shape
    cos, sa, sb = rope
    half = GROUP // 2
    qg = jnp.tile(q_gain, GROUP // HEAD_DIM).reshape(1, GROUP)
    kg = jnp.tile(k_gain, half // HEAD_DIM).reshape(1, half)
    tile_spec = pl.BlockSpec((TILE, GROUP), lambda i, j: (j, 0))
    full_spec = pl.BlockSpec((t, half), lambda i, j: (0, 0))
    return pl.pallas_call(
        functools.partial(_attn_c_kernel, tc=tc),
        grid=(b, t // TILE),
        in_specs=[pl.BlockSpec((1, TILE, GROUP), lambda i, j: (i, j, COL_C_Q // GROUP)),
                  pl.BlockSpec((1, t, half), lambda i, j: (i, 0, COL_C_K // half)),
                  pl.BlockSpec((1, t, half), lambda i, j: (i, 0, COL_C_V // half)),
                  tile_spec, tile_spec, tile_spec, full_spec, full_spec, full_spec,
                  pl.BlockSpec((1, GROUP), lambda i, j: (0, 0)),
                  pl.BlockSpec((1, half), lambda i, j: (0, 0)),
                  pl.BlockSpec((GROUP, GROUP), lambda i, j: (0, 0))],
        out_specs=pl.BlockSpec((1, TILE, GROUP), lambda i, j: (i, j, 0)),
        out_shape=jax.ShapeDtypeStruct((b, t, GROUP), F32),
        scratch_shapes=[pltpu.VMEM((t, half), BF16),
                        pltpu.VMEM((half // HEAD_DIM, V_ROWS, t), BF16),
                        pltpu.VMEM((2, t, 2 * TILE), F32)],
        compiler_params=_cparams(2),
        name="gqa_attention",
    )(proj, proj, proj, cos, sa, sb, cos, sa, sb, qg, kg, bd)


def _fourier_kernel(u_ref, c64_ref, dctx_ref, dx_ref, o_ref, uc_ref, ux_ref, *, tc):
    t = pl.program_id(1)
    last = pl.num_programs(1) - 1
    s = ux_ref.shape[0] // 2

    @pl.when(t == 0)
    def _():
        w = _dot(u_ref[0, 0:s, :].astype(BF16), c64_ref[...])
        ux_ref[0:s, :] = w[:, 0:GROUP].astype(BF16)
        ux_ref[s:2 * s, :] = w[:, GROUP:2 * GROUP].astype(BF16)

    @pl.when(t < last)
    def _():
        o_ref[0] = _dot(dx_ref[...], ux_ref[...]) * (s ** -0.5)

    @pl.when(t == last)
    def _():
        w = _dot(u_ref[0, s:s + tc, :].astype(BF16), c64_ref[...])
        uc_ref[0:tc, :] = w[:, 0:GROUP].astype(BF16)
        uc_ref[tc:2 * tc, :] = w[:, GROUP:2 * GROUP].astype(BF16)
        o_ref[0] = _dot(dctx_ref[...], uc_ref[...]) * (tc ** -0.5)


def _fourier(proj, c64, dctx, dx, tc):
    b, t, _ = proj.shape
    s = t - tc
    return pl.pallas_call(
        functools.partial(_fourier_kernel, tc=tc),
        grid=(b, t // TILE),
        in_specs=[pl.BlockSpec((1, t, GROUP), lambda i, j: (i, 0, COL_D_U // GROUP)),
                  pl.BlockSpec((GROUP, 2 * GROUP), lambda i, j: (0, 0)),
                  pl.BlockSpec((tc, 2 * tc), lambda i, j: (0, 0)),
                  pl.BlockSpec((TILE, 2 * s), lambda i, j: (jnp.minimum(j, s // TILE - 1), 0))],
        out_specs=pl.BlockSpec((1, TILE, GROUP), lambda i, j: (i, j, 0)),
        out_shape=jax.ShapeDtypeStruct((b, t, GROUP), F32),
        scratch_shapes=[pltpu.VMEM((2 * tc, GROUP), BF16), pltpu.VMEM((2 * s, GROUP), BF16)],
        compiler_params=_cparams(2),
        name="fourier",
    )(proj, c64, dctx, dx)


def _dft_tables(tc, s):
    def cs(n):
        k = jnp.arange(n, dtype=jnp.int32)
        ang = ((k[:, None] * k[None, :]) % n).astype(F32) * (2.0 * math.pi / n)
        return jnp.cos(ang), jnp.sin(ang)

    c64, s64 = cs(HEAD_DIM)
    eye = jnp.eye(GROUP // HEAD_DIM, dtype=F32)
    c64 = jnp.concatenate([jnp.kron(eye, c64), jnp.kron(eye, s64)], axis=1) * (HEAD_DIM ** -0.5)
    cc, sc = cs(tc)
    cx, sx = cs(s)
    return (c64.astype(BF16), jnp.concatenate([cc, -sc], axis=1).astype(BF16),
            jnp.concatenate([cx, -sx], axis=1).astype(BF16))


def _hgrn_tables():
    c = CHUNK
    idx = np.arange(c)
    tri = (idx[None, :] <= idx[:, None]).astype(np.float32)
    lvl = np.full((c, c), -1.0, np.float32)
    lvl[idx, idx] = 0.0
    rows = [tri]
    for l in range(1, LEVELS + 1):
        w = c >> (l - 1)
        blk, pos = idx // w, idx % w
        ref = blk * w + w // 2 - 1
        rows.append(tri[ref])
        m = (blk[:, None] == blk[None, :]) & (pos[:, None] >= w // 2) & (pos[None, :] < w // 2)
        lvl[m] = float(l)
    w_f = np.concatenate(rows, axis=0)
    w_b = w_f.reshape(LEVELS + 1, c, c)[:, ::-1, ::-1].reshape(-1, c)
    heads = GROUP // HEAD_DIM
    head_mask = np.kron(np.eye(heads, dtype=np.float32), np.ones((c, HEAD_DIM), np.float32))
    return (np.tile(w_f, (1, 3)), np.tile(w_b, (1, 3)),
            np.tile(lvl, (1, heads)), np.tile(lvl.T, (1, heads)), head_mask)


def _hgrn_chunk(q, z, v, lb, w3, lvl, hm, st_ref, end_row):
    c = CHUNK
    e = jnp.exp(-jnp.abs(z))
    r = 1.0 / (1.0 + e)
    sig_pos = jnp.where(z >= 0, r, e * r)
    sig_neg = jnp.where(z >= 0, e * r, r)
    g = jnp.log(lb + (1.0 - lb) * sig_pos)
    k = (1.0 - lb) * sig_neg
    g_hi = g.astype(BF16)
    r1 = g - g_hi.astype(F32)
    g_mid = r1.astype(BF16)
    g_lo = (r1 - g_mid.astype(F32)).astype(BF16)
    sums = _dot(w3, jnp.concatenate([g_hi, g_mid, g_lo], axis=0))
    b = sums[0:c]
    b_end = b[end_row:end_row + 1]

    def block_diag(x):
        return jnp.where(hm > 0.5, jnp.concatenate([x] * (GROUP // HEAD_DIM), axis=0), 0.0)

    a = jnp.where(lvl == 0.0, _dot_nt(q.astype(BF16), block_diag(k).astype(BF16)), 0.0)
    for l in range(1, LEVELS + 1):
        d = jnp.exp(-jnp.abs(b - sums[l * c:(l + 1) * c]))
        a_l = _dot_nt((q * d).astype(BF16), block_diag(k * d).astype(BF16))
        a = jnp.where(lvl == float(l), a_l, a)
    st = st_ref[...]
    o = _dot(a.astype(BF16), block_diag(v).astype(BF16))
    o = o + _dot_nt((q * jnp.exp(b)).astype(BF16), st.astype(BF16))
    upd = _dot_tn(v.astype(BF16), (k * jnp.exp(b_end - b)).astype(BF16))
    st_ref[...] = st * jnp.exp(b_end) + jnp.where(hm > 0.5, upd, 0.0)
    return o


def _hgrn_kernel(qf_ref, zf_ref, vf_ref, qb_ref, zb_ref, vb_ref, lbf_ref, lbb_ref,
                 w3f_ref, w3b_ref, lvf_ref, lvb_ref, hm_ref, of_ref, ob_ref, sf_ref, sb_ref):
    @pl.when(pl.program_id(1) == 0)
    def _():
        sf_ref[...] = jnp.zeros_like(sf_ref)
        sb_ref[...] = jnp.zeros_like(sb_ref)

    hm = hm_ref[...]
    n = TILE // CHUNK
    for ci in range(n):
        rows = slice(ci * CHUNK, (ci + 1) * CHUNK)
        of_ref[0, rows, :] = _hgrn_chunk(qf_ref[0, rows, :], zf_ref[0, rows, :], vf_ref[0, rows, :],
                                         lbf_ref[...], w3f_ref[...], lvf_ref[...], hm, sf_ref,
                                         CHUNK - 1)
        rows = slice((n - 1 - ci) * CHUNK, (n - ci) * CHUNK)
        ob_ref[0, rows, :] = _hgrn_chunk(qb_ref[0, rows, :], zb_ref[0, rows, :], vb_ref[0, rows, :],
                                         lbb_ref[...], w3b_ref[...], lvb_ref[...], hm, sb_ref, 0)


def _hgrn(proj, lb_f, lb_b, tables):
    b, t, _ = proj.shape
    nt = t // TILE
    w3f, w3b, lvf, lvb, hm = tables

    def fwd(col):
        return pl.BlockSpec((1, TILE, GROUP),
                            lambda i, j: (i, jnp.where(j == 0, nt - 1, j - 1), col // GROUP))

    def bwd(col):
        return pl.BlockSpec((1, TILE, GROUP), lambda i, j: (i, nt - 1 - j, col // GROUP))

    def const(a):
        return pl.BlockSpec(a.shape, lambda i, j: (0,) * a.ndim)

    out_f = fwd(0)
    out_b = bwd(0)
    lb_f = lb_f.reshape(1, GROUP)
    lb_b = lb_b.reshape(1, GROUP)
    return pl.pallas_call(
        _hgrn_kernel,
        grid=(b, nt),
        in_specs=[fwd(COL_B_Q), fwd(COL_B_FF), fwd(COL_B_I), bwd(COL_B_Q), bwd(COL_B_FB), bwd(COL_B_I),
                  const(lb_f), const(lb_b), const(w3f), const(w3b), const(lvf), const(lvb), const(hm)],
        out_specs=[out_f, out_b],
        out_shape=[jax.ShapeDtypeStruct((b, t, GROUP), F32)] * 2,
        scratch_shapes=[pltpu.VMEM((GROUP, GROUP), F32), pltpu.VMEM((GROUP, GROUP), F32)],
        compiler_params=_cparams(2),
        name="hgrn2",
    )(proj, proj, proj, proj, proj, proj, lb_f, lb_b, w3f, w3b, lvf, lvb, hm)


def _outffn_kernel(x_ref, ma_ref, of_ref, ob_ref, g_ref, mc_ref, md_ref, mod_ref, hg_ref, n2_ref,
                   fn_ref, bd_ref, wo_ref, wi_ref, wd_ref, o_ref, act_ref, *, hidden, final):
    n_sub = x_ref.shape[1] // TILE
    o = of_ref[0] + ob_ref[0]
    g = g_ref[0]
    mb = (o * lax.rsqrt(_group_mean_sq(o, bd_ref[...]) + RMS_EPS) * hg_ref[...]) * (g * jax.nn.sigmoid(g))
    mix = jnp.concatenate([ma_ref[0], mb, mc_ref[0], md_ref[0]], axis=-1).astype(BF16)
    y = _dot(mix, wo_ref[...])
    xs, hs, gates = [], [], []
    for i in range(n_sub):
        m = _mod_rows(mod_ref, i, n_sub, not final)
        x = x_ref[0, i * TILE:(i + 1) * TILE, :] + m[2:3] * y[i * TILE:(i + 1) * TILE]
        ms = jnp.mean(x * x, axis=-1, keepdims=True)
        xs.append(x)
        hs.append((x * lax.rsqrt(ms + RMS_EPS) * (n2_ref[...] * (1.0 + m[4:5])) + m[3:4]).astype(BF16))
        gates.append(m[5:6])
    h = jnp.concatenate(hs, axis=0)
    step = 256
    for c in range(hidden // step):
        gate = _dot(h, wi_ref[:, c * step:(c + 1) * step])
        up = _dot(h, wi_ref[:, hidden + c * step:hidden + (c + 1) * step])
        act_ref[:, c * step:(c + 1) * step] = (gate * jax.nn.sigmoid(gate) * up).astype(BF16)
    acc = _dot(act_ref[...], wd_ref[...])
    for i in range(n_sub):
        x = xs[i] + gates[i] * acc[i * TILE:(i + 1) * TILE]
        if final:
            ms = jnp.mean(x * x, axis=-1, keepdims=True)
            x = x * lax.rsqrt(ms + RMS_EPS) * fn_ref[...]
        o_ref[0, i * TILE:(i + 1) * TILE, :] = x


def _out_ffn(xs, ma, o_f, o_b, proj, mc, md, modsel, hgrn_gain, norm2, final_norm, bd, w_out,
             w_ffn_in, w_ffn_out, n_rows, final):
    b, _, d = xs.shape
    hidden = w_ffn_out.shape[0]
    tm = _row_tile(n_rows) if not final else 2 * TILE

    def rows(width, col=0):
        return pl.BlockSpec((1, tm, width), lambda i, j: (i, j, col))

    hg = jnp.tile(hgrn_gain, GROUP // HEAD_DIM).reshape(1, GROUP)
    return pl.pallas_call(
        functools.partial(_outffn_kernel, hidden=hidden, final=final),
        grid=(b, n_rows // tm),
        in_specs=[rows(d), rows(GROUP), rows(GROUP), rows(GROUP), rows(GROUP, COL_B_G // GROUP),
                  rows(GROUP), rows(GROUP),
                  pl.BlockSpec((1, 2, 6, d), lambda i, j: (i, 0, 0, 0)),
                  _resident((1, GROUP)), _resident((1, d)), _resident((1, d)),
                  _resident((GROUP, GROUP)), _resident(w_out.shape), _resident(w_ffn_in.shape),
                  _resident(w_ffn_out.shape)],
        out_specs=rows(d),
        out_shape=jax.ShapeDtypeStruct((b, n_rows, d), F32),
        scratch_shapes=[pltpu.VMEM((tm, hidden), BF16)],
        compiler_params=_cparams(2),
        name="out_ffn",
    )(xs, ma, o_f, o_b, proj, mc, md, modsel, hg, norm2.reshape(1, d), final_norm.reshape(1, d), bd,
      w_out, w_ffn_in, w_ffn_out)


def _rope_tables(tc, s, dim, width):
    pos = jnp.arange(s)
    row = (pos // GRID_W).astype(F32)
    col = (pos % GRID_W).astype(F32)
    n_freq = dim // 4
    inv_freq = ROPE_THETA ** (-jnp.arange(n_freq, dtype=F32) / n_freq)
    lane = np.arange(width) % dim
    is_col = (lane // (dim // 2)) == 1
    second = ((lane % (dim // 2)) // n_freq) == 1
    freq = inv_freq[lane % n_freq]
    ang = jnp.where(is_col[None, :], col[:, None], row[:, None]) * freq[None, :]
    cos, sin = jnp.cos(ang), jnp.sin(ang)
    sin_a = jnp.where(second[None, :], sin, 0.0)
    sin_b = jnp.where(second[None, :], 0.0, -sin)
    ident = jnp.ones((tc, width), F32)
    zero = jnp.zeros((tc, width), F32)
    return (jnp.concatenate([cos, ident], axis=0), jnp.concatenate([sin_a, zero], axis=0),
            jnp.concatenate([sin_b, zero], axis=0))


def kernel(x, c, ctx, c_ctx, w_mod, b_mod, norm1, w_in, diff_lambda, diff_norm, hgrn_lb_logits,
           hgrn_norm, q_norm, k_norm, w_out, norm2, w_ffn_in, w_ffn_out, final_norm):
    b, s, d = x.shape
    tc = ctx.shape[1]
    depth = w_mod.shape[0]
    assert tc == TILE and s % TILE == 0 and s % GRID_W == 0 and d == 4 * GROUP

    rope_a = _rope_tables(tc, s, A_QK, GROUP)
    rope_c = _rope_tables(tc, s, HEAD_DIM, GROUP)
    dft = _dft_tables(tc, s)
    hgrn_tabs = _hgrn_tables()
    hgrn_tabs = tuple(jnp.asarray(a, BF16 if i < 2 else F32) for i, a in enumerate(hgrn_tabs))
    bd = jnp.asarray(np.kron(np.eye(GROUP // HEAD_DIM), np.full((HEAD_DIM, HEAD_DIM), 1.0 / HEAD_DIM)), BF16)

    p = jax.nn.softmax(hgrn_lb_logits.astype(F32), axis=1)
    lower = jnp.cumsum(p, axis=1) - p[:, :1]

    hp = np.arange(GROUP).reshape(4, HEAD_DIM)[[0, 2, 1, 3]].reshape(-1)
    w_in_b = w_in.astype(BF16)
    w_in_b = w_in_b.at[:, :, COL_C_Q:COL_C_Q + GROUP].set(w_in_b[:, :, COL_C_Q + hp])
    w_out_b = w_out.astype(BF16)
    w_out_b = w_out_b.at[:, 2 * GROUP:3 * GROUP, :].set(w_out_b[:, 2 * GROUP + hp, :])
    w_ffn_in_b = w_ffn_in.astype(BF16)
    w_ffn_out_b = w_ffn_out.astype(BF16)

    rows = -(-(b + 1) // 8) * 8
    cc = jnp.zeros((rows, d), F32).at[:b].set(c).at[b].set(c_ctx)
    mod = _modulation(cc, w_mod, b_mod).reshape(depth, rows, 6, d)
    modsel = jnp.stack([jnp.broadcast_to(mod[:, b:b + 1], (depth, b, 6, d)), mod[:, :b]], axis=2)

    xs = jnp.concatenate([x, ctx], axis=1)
    for l in range(depth):
        last = l == depth - 1
        lam_init = 0.8 - 0.6 * math.exp(-0.3 * l)
        proj = _in_proj(xs, modsel[l], norm1[l], w_in_b[l])
        m_a = _attn_a(proj, rope_a, diff_lambda[l], diff_norm[l], bd, tc, lam_init)
        o_f, o_b = _hgrn(proj, lower[0, l], lower[1, l], hgrn_tabs)
        m_c = _attn_c(proj, rope_c, q_norm[l], k_norm[l], bd, tc)
        m_d = _fourier(proj, *dft, tc)
        xs = _out_ffn(xs, m_a, o_f, o_b, proj, m_c, m_d, modsel[l], hgrn_norm[l], norm2[l], final_norm,
                      bd, w_out_b[l], w_ffn_in_b[l], w_ffn_out_b[l], s if last else s + tc, last)
    return xs
```

```python
import functools
import math

import numpy as np
import jax
import jax.numpy as jnp
from jax import lax
from jax.experimental import pallas as pl
from jax.experimental.pallas import tpu as pltpu

F32 = jnp.float32
BF16 = jnp.bfloat16

HEAD_DIM = 64
GRID_W = 64
ROPE_THETA = 10000.0
RMS_EPS = 1e-6
GROUP = 256
A_QK = 32
TILE = 256
CHUNK = 64
LEVELS = 6
LOG2E = 1.4426950408889634
VMEM_LIMIT = 56 * 1024 * 1024

COL_A_Q, COL_A_K, COL_A_V = 0, 256, 512
COL_B_Q, COL_B_FF, COL_B_FB, COL_B_I, COL_B_G = 768, 1024, 1280, 1536, 1792
COL_C_Q, COL_C_K, COL_C_V = 2048, 2304, 2432
COL_D_U = 2560


def _cparams(n_axes):
    return pltpu.CompilerParams(dimension_semantics=("arbitrary",) * n_axes,
                                vmem_limit_bytes=VMEM_LIMIT)


def _dot(a, b):
    return jnp.dot(a, b, preferred_element_type=F32)


def _dot_nt(a, b):
    return lax.dot_general(a, b, (((1,), (1,)), ((), ())), preferred_element_type=F32)


def _dot_tn(a, b):
    return lax.dot_general(a, b, (((0,), (0,)), ((), ())), preferred_element_type=F32)


def _group_mean_sq(y, bd):
    y2 = y * y
    hi = y2.astype(BF16)
    lo = (y2 - hi.astype(F32)).astype(BF16)
    return _dot(hi, bd) + _dot(lo, bd)


def _rope(x, cos, sin_a, sin_b, shift):
    w = x.shape[-1]
    return x * cos + pltpu.roll(x, shift, 1) * sin_a + pltpu.roll(x, w - shift, 1) * sin_b


def _mod_kernel(c_ref, w_ref, b_ref, o_ref):
    c = c_ref[...]
    a = c * jax.nn.sigmoid(c)
    o_ref[0] = _dot(a.astype(BF16), w_ref[0].astype(BF16)) + b_ref[0]


def _modulation(cc, w_mod, b_mod):
    depth, d, n = w_mod.shape
    r = cc.shape[0]
    tn = 1536
    return pl.pallas_call(
        _mod_kernel,
        grid=(depth, n // tn),
        in_specs=[pl.BlockSpec((r, d), lambda l, j: (0, 0)),
                  pl.BlockSpec((1, d, tn), lambda l, j: (l, 0, j)),
                  pl.BlockSpec((1, 1, tn), lambda l, j: (l, 0, j))],
        out_specs=pl.BlockSpec((1, r, tn), lambda l, j: (l, 0, j)),
        out_shape=jax.ShapeDtypeStruct((depth, r, n), F32),
        compiler_params=_cparams(2),
        name="modulation",
    )(cc, w_mod, b_mod.reshape(depth, 1, n))


def _mod_rows(mod_ref, sub, n_sub, with_ctx):
    m = mod_ref[0]
    if not (with_ctx and sub == n_sub - 1):
        return m[1]
    return jnp.where(pl.program_id(1) == pl.num_programs(1) - 1, m[0], m[1])


def _inproj_kernel(x_ref, mod_ref, n1_ref, w_ref, o_ref):
    n_sub = x_ref.shape[1] // TILE
    hs = []
    for i in range(n_sub):
        x = x_ref[0, i * TILE:(i + 1) * TILE, :]
        ms = jnp.mean(x * x, axis=-1, keepdims=True)
        y = x * lax.rsqrt(ms + RMS_EPS) * n1_ref[...]
        m = _mod_rows(mod_ref, i, n_sub, True)
        hs.append((y * (1.0 + m[1:2]) + m[0:1]).astype(BF16))
    o_ref[0] = _dot(jnp.concatenate(hs, axis=0), w_ref[...])


def _row_tile(t):
    return next(tm for tm in (3 * TILE, 2 * TILE, TILE) if t % tm == 0)


def _resident(shape):
    return pl.BlockSpec(shape, lambda i, j: (0,) * len(shape), pipeline_mode=pl.Buffered(1))


def _in_proj(xs, modsel, norm1, w_in):
    b, t, d = xs.shape
    n = w_in.shape[1]
    tm = _row_tile(t)
    return pl.pallas_call(
        _inproj_kernel,
        grid=(b, t // tm),
        in_specs=[pl.BlockSpec((1, tm, d), lambda i, j: (i, j, 0)),
                  pl.BlockSpec((1, 2, 6, d), lambda i, j: (i, 0, 0, 0)),
                  _resident((1, d)), _resident((d, n))],
        out_specs=pl.BlockSpec((1, tm, n), lambda i, j: (i, j, 0)),
        out_shape=jax.ShapeDtypeStruct((b, t, n), F32),
        compiler_params=_cparams(2),
        name="in_proj",
    )(xs, modsel, norm1.reshape(1, d), w_in)


V_ROWS = HEAD_DIM + 16


KEY_TILE = 256


def _attend_t(kr_ref, vt_ref, s_ref, wqs, vt_index, lo, nk):
    n_kt = nk // KEY_TILE
    n_g = len(wqs)
    maxes = [None] * n_g
    accs = [None] * n_g

    def keys(kt):
        return slice(lo + kt * KEY_TILE, lo + (kt + 1) * KEY_TILE)

    def score_tile(g, kt):
        s = _dot_nt(kr_ref[keys(kt), :], wqs[g])
        s_ref[g % 2, kt * KEY_TILE:(kt + 1) * KEY_TILE, :] = s
        m = jnp.max(s, axis=0, keepdims=True)
        maxes[g] = m if kt == 0 else jnp.maximum(maxes[g], m)

    def value_tile(g, kt, p):
        pv = _dot(vt_ref[vt_index[g], :, keys(kt)], p)
        accs[g] = pv if kt == 0 else accs[g] + pv

    for kt in range(n_kt):
        score_tile(0, kt)
    for g in range(n_g):
        p_prev = None
        for kt in range(n_kt):
            if g + 1 < n_g:
                score_tile(g + 1, kt)
            p = jnp.exp2(s_ref[g % 2, kt * KEY_TILE:(kt + 1) * KEY_TILE, :] - maxes[g]).astype(BF16)
            if kt > 0:
                value_tile(g, kt - 1, p_prev)
            p_prev = p
        value_tile(g, n_kt - 1, p_prev)
    return [a[0:HEAD_DIM] * (1.0 / a[HEAD_DIM:HEAD_DIM + 1]) for a in accs]


def _store_values_t(vt_ref, v):
    v_t = v.T
    for h in range(vt_ref.shape[0]):
        vt_ref[h, 0:HEAD_DIM, :] = v_t[h * HEAD_DIM:(h + 1) * HEAD_DIM, :].astype(BF16)
        vt_ref[h, HEAD_DIM:V_ROWS, :] = jnp.ones((V_ROWS - HEAD_DIM, v.shape[0]), BF16)


def _attn_a_kernel(q_ref, k_ref, v_ref, cq_ref, saq_ref, sbq_ref, ck_ref, sak_ref, sbk_ref,
                   lp_ref, gain_ref, bd_ref, o_ref, kr_ref, vt_ref, s_ref, *, tc, lam_init):
    t = pl.program_id(1)
    n_tok = kr_ref.shape[0]

    @pl.when(t == 0)
    def _():
        kr = _rope(k_ref[0], ck_ref[...], sak_ref[...], sbk_ref[...], A_QK // 4)
        kr_ref[...] = kr.astype(BF16)
        _store_values_t(vt_ref, v_ref[0])

    qr = _rope(q_ref[0], cq_ref[...], saq_ref[...], sbq_ref[...], A_QK // 4)
    qr = qr * (A_QK ** -0.5 * LOG2E)
    lp = lp_ref[...]
    lam = (jnp.exp(jnp.sum(lp[0:1] * lp[1:2], axis=-1, keepdims=True))
           - jnp.exp(jnp.sum(lp[2:3] * lp[3:4], axis=-1, keepdims=True)) + lam_init)
    lane = lax.broadcasted_iota(jnp.int32, (1, GROUP), 1)

    def attend(lo, nk):
        n_heads = GROUP // HEAD_DIM
        wqs = [jnp.concatenate([jnp.where(lane // A_QK == 2 * h + c, qr, 0.0).astype(BF16)
                                for c in range(2)], axis=0) for h in range(n_heads)]
        outs = _attend_t(kr_ref, vt_ref, s_ref, wqs, list(range(n_heads)), lo, nk)
        heads = [o[:, 0:TILE] - lam * o[:, TILE:2 * TILE] for o in outs]
        acc = jnp.concatenate(heads, axis=0).T
        y = acc * lax.rsqrt(_group_mean_sq(acc, bd_ref[...]) + RMS_EPS)
        o_ref[0] = y * gain_ref[...] * (1.0 - lam_init)

    @pl.when(t == pl.num_programs(1) - 1)
    def _():
        attend(n_tok - tc, tc)

    @pl.when(t < pl.num_programs(1) - 1)
    def _():
        attend(0, n_tok)


def _attn_a(proj, rope, lam_params, sub_gain, bd, tc, lam_init):
    b, t, _ = proj.shape
    cos, sa, sb = rope
    gain = jnp.tile(sub_gain, GROUP // HEAD_DIM).reshape(1, GROUP)
    tile_spec = pl.BlockSpec((TILE, GROUP), lambda i, j: (j, 0))
    full_spec = pl.BlockSpec((t, GROUP), lambda i, j: (0, 0))
    kern = functools.partial(_attn_a_kernel, tc=tc, lam_init=lam_init)
    return pl.pallas_call(
        kern,
        grid=(b, t // TILE),
        in_specs=[pl.BlockSpec((1, TILE, GROUP), lambda i, j: (i, j, COL_A_Q // GROUP)),
                  pl.BlockSpec((1, t, GROUP), lambda i, j: (i, 0, COL_A_K // GROUP)),
                  pl.BlockSpec((1, t, GROUP), lambda i, j: (i, 0, COL_A_V // GROUP)),
                  tile_spec, tile_spec, tile_spec, full_spec, full_spec, full_spec,
                  pl.BlockSpec(lam_params.shape, lambda i, j: (0, 0)),
                  pl.BlockSpec((1, GROUP), lambda i, j: (0, 0)),
                  pl.BlockSpec((GROUP, GROUP), lambda i, j: (0, 0))],
        out_specs=pl.BlockSpec((1, TILE, GROUP), lambda i, j: (i, j, 0)),
        out_shape=jax.ShapeDtypeStruct((b, t, GROUP), F32),
        scratch_shapes=[pltpu.VMEM((t, GROUP), BF16),
                        pltpu.VMEM((GROUP // HEAD_DIM, V_ROWS, t), BF16),
                        pltpu.VMEM((2, t, 2 * TILE), F32)],
        compiler_params=_cparams(2),
        name="diff_attention",
    )(proj, proj, proj, cos, sa, sb, cos, sa, sb, lam_params, gain, bd)


def _attn_c_kernel(q_ref, k_ref, v_ref, cq_ref, saq_ref, sbq_ref, ck_ref, sak_ref, sbk_ref,
                   qg_ref, kg_ref, bd_ref, o_ref, kr_ref, vt_ref, s_ref, *, tc):
    t = pl.program_id(1)
    n_tok = kr_ref.shape[0]
    half = GROUP // 2
    bd = bd_ref[...]

    @pl.when(t == 0)
    def _():
        k = k_ref[0]
        kn = k * lax.rsqrt(_group_mean_sq(k, bd[0:half, 0:half]) + RMS_EPS) * kg_ref[...]
        kr_ref[...] = _rope(kn, ck_ref[...], sak_ref[...], sbk_ref[...], HEAD_DIM // 4).astype(BF16)
        _store_values_t(vt_ref, v_ref[0])

    q = q_ref[0]
    qn = q * lax.rsqrt(_group_mean_sq(q, bd) + RMS_EPS) * qg_ref[...]
    qr = _rope(qn, cq_ref[...], saq_ref[...], sbq_ref[...], HEAD_DIM // 4) * (HEAD_DIM ** -0.5 * LOG2E)
    lane = lax.broadcasted_iota(jnp.int32, (1, half), 1)

    def attend(lo, nk):
        wqs = [jnp.concatenate([jnp.where(lane // HEAD_DIM == g, qr[:, qb * half:(qb + 1) * half], 0.0)
                                for qb in range(2)], axis=0).astype(BF16) for g in range(2)]
        outs = _attend_t(kr_ref, vt_ref, s_ref, wqs, [0, 1], lo, nk)
        heads = [outs[g][:, qb * TILE:(qb + 1) * TILE] for qb in range(2) for g in range(2)]
        o_ref[0] = jnp.concatenate(heads, axis=0).T

    @pl.when(t == pl.num_programs(1) - 1)
    def _():
        attend(n_tok - tc, tc)

    @pl.when(t < pl.num_programs(1) - 1)
    def _():
        attend(0, n_tok)


def _attn_c(proj, rope, q_gain, k_gain, bd, tc):
    b, t, _ = proj.shape
    cos, sa, sb = rope
    half = GROUP // 2
    qg = jnp.tile(q_gain, GROUP // HEAD_DIM).reshape(1, GROUP)
    kg = jnp.tile(k_gain, half // HEAD_DIM).reshape(1, half)
    tile_spec = pl.BlockSpec((TILE, GROUP), lambda i, j: (j, 0))
    full_spec = pl.BlockSpec((t, half), lambda i, j: (0, 0))
    return pl.pallas_call(
        functools.partial(_attn_c_kernel, tc=tc),
        grid=(b, t // TILE),
        in_specs=[pl.BlockSpec((1, TILE, GROUP), lambda i, j: (i, j, COL_C_Q // GROUP)),
                  pl.BlockSpec((1, t, half), lambda i, j: (i, 0, COL_C_K // half)),
                  pl.BlockSpec((1, t, half), lambda i, j: (i, 0, COL_C_V // half)),
                  tile_spec, tile_spec, tile_spec, full_spec, full_spec, full_spec,
                  pl.BlockSpec((1, GROUP), lambda i, j: (0, 0)),
                  pl.BlockSpec((1, half), lambda i, j: (0, 0)),
                  pl.BlockSpec((GROUP, GROUP), lambda i, j: (0, 0))],
        out_specs=pl.BlockSpec((1, TILE, GROUP), lambda i, j: (i, j, 0)),
        out_shape=jax.ShapeDtypeStruct((b, t, GROUP), F32),
        scratch_shapes=[pltpu.VMEM((t, half), BF16),
                        pltpu.VMEM((half // HEAD_DIM, V_ROWS, t), BF16),
                        pltpu.VMEM((2, t, 2 * TILE), F32)],
        compiler_params=_cparams(2),
        name="gqa_attention",
    )(proj, proj, proj, cos, sa, sb, cos, sa, sb, qg, kg, bd)


def _fourier_stage_kernel(ux_ref, uc_ref, c64_ref, dctx_ref, w_ref, o_ref):
    s = ux_ref.shape[1]
    tc = uc_ref.shape[1]
    w = _dot(ux_ref[0].astype(BF16), c64_ref[...])
    w_ref[0:s, :] = w[:, 0:GROUP].astype(BF16)
    w_ref[s:2 * s, :] = w[:, GROUP:2 * GROUP].astype(BF16)
    wc = _dot(uc_ref[0].astype(BF16), c64_ref[...])
    wc = jnp.concatenate([wc[:, 0:GROUP], wc[:, GROUP:2 * GROUP]], axis=0).astype(BF16)
    o_ref[0] = _dot(dctx_ref[...], wc) * (tc ** -0.5)


def _fourier_dft_kernel(d_ref, w_ref, alias_ref, o_ref):
    del alias_ref
    o_ref[0] = _dot(d_ref[...], w_ref[...]) * ((d_ref.shape[1] // 2) ** -0.5)


def _fourier(proj, c64, dctx, dx, tc):
    b, t, _ = proj.shape
    s = t - tc
    col = COL_D_U // GROUP
    w_all, o_ctx = pl.pallas_call(
        _fourier_stage_kernel,
        grid=(b,),
        in_specs=[pl.BlockSpec((1, s, GROUP), lambda i: (i, 0, col)),
                  pl.BlockSpec((1, tc, GROUP), lambda i: (i, s // tc, col)),
                  pl.BlockSpec((GROUP, 2 * GROUP), lambda i: (0, 0)),
                  pl.BlockSpec((tc, 2 * tc), lambda i: (0, 0))],
        out_specs=[pl.BlockSpec((2 * s, GROUP), lambda i: (0, i)),
                   pl.BlockSpec((1, tc, GROUP), lambda i: (i, s // tc, 0))],
        out_shape=[jax.ShapeDtypeStruct((2 * s, b * GROUP), BF16),
                   jax.ShapeDtypeStruct((b, t, GROUP), F32)],
        compiler_params=_cparams(1),
        name="fourier_stage",
    )(proj, proj, c64, dctx)
    tm = 2 * TILE
    return pl.pallas_call(
        _fourier_dft_kernel,
        grid=(s // tm, b),
        in_specs=[pl.BlockSpec((tm, 2 * s), lambda i, j: (i, 0)),
                  pl.BlockSpec((2 * s, GROUP), lambda i, j: (0, j)),
                  pl.BlockSpec(memory_space=pl.ANY)],
        out_specs=pl.BlockSpec((1, tm, GROUP), lambda i, j: (j, i, 0)),
        out_shape=jax.ShapeDtypeStruct((b, t, GROUP), F32),
        input_output_aliases={2: 0},
        compiler_params=_cparams(2),
        name="fourier_dft",
    )(dx, w_all, o_ctx)


def _dft_tables(tc, s):
    def cs(n):
        k = jnp.arange(n, dtype=jnp.int32)
        ang = ((k[:, None] * k[None, :]) % n).astype(F32) * (2.0 * math.pi / n)
        return jnp.cos(ang), jnp.sin(ang)

    c64, s64 = cs(HEAD_DIM)
    eye = jnp.eye(GROUP // HEAD_DIM, dtype=F32)
    c64 = jnp.concatenate([jnp.kron(eye, c64), jnp.kron(eye, s64)], axis=1) * (HEAD_DIM ** -0.5)
    cc, sc = cs(tc)
    cx, sx = cs(s)
    return (c64.astype(BF16), jnp.concatenate([cc, -sc], axis=1).astype(BF16),
            jnp.concatenate([cx, -sx], axis=1).astype(BF16))


def _hgrn_tables():
    c = CHUNK
    idx = np.arange(c)
    tri = (idx[None, :] <= idx[:, None]).astype(np.float32)
    lvl = np.full((c, c), -1.0, np.float32)
    lvl[idx, idx] = 0.0
    rows = [tri]
    for l in range(1, LEVELS + 1):
        w = c >> (l - 1)
        blk, pos = idx // w, idx % w
        ref = blk * w + w // 2 - 1
        rows.append(tri[ref])
        m = (blk[:, None] == blk[None, :]) & (pos[:, None] >= w // 2) & (pos[None, :] < w // 2)
        lvl[m] = float(l)
    w_f = np.concatenate(rows, axis=0)
    w_b = w_f.reshape(LEVELS + 1, c, c)[:, ::-1, ::-1].reshape(-1, c)
    heads = GROUP // HEAD_DIM
    head_mask = np.kron(np.eye(heads, dtype=np.float32), np.ones((c, HEAD_DIM), np.float32))
    return (np.tile(w_f, (1, 3)), np.tile(w_b, (1, 3)),
            np.tile(lvl, (1, heads)), np.tile(lvl.T, (1, heads)), head_mask)


HGRN_FAST_RANGE = 75.0


def _hgrn_gates(z, lb):
    e = jnp.exp(-jnp.abs(z))
    r = 1.0 / (1.0 + e)
    sig_pos = jnp.where(z >= 0, r, e * r)
    sig_neg = jnp.where(z >= 0, e * r, r)
    return jnp.log(lb + (1.0 - lb) * sig_pos), (1.0 - lb) * sig_neg


def _split3(g):
    g_hi = g.astype(BF16)
    r1 = g - g_hi.astype(F32)
    g_mid = r1.astype(BF16)
    g_lo = (r1 - g_mid.astype(F32)).astype(BF16)
    return jnp.concatenate([g_hi, g_mid, g_lo], axis=0)


def _block_diag(x, hm_b):
    return jnp.concatenate([x.astype(BF16)] * (GROUP // HEAD_DIM), axis=0) * hm_b


def _hgrn_output(ch, a, hm, hm_b, st_ref, end_row):
    q, k, v, b = ch["q"], ch["k"], ch["v"], ch["b"]
    b_end = b[end_row:end_row + 1]
    st = st_ref[...]
    o = _dot(a.astype(BF16), _block_diag(v, hm_b))
    o = o + _dot_nt((q * jnp.exp(b)).astype(BF16), st.astype(BF16))
    upd = _dot_tn(v.astype(BF16), (k * jnp.exp(b_end - b)).astype(BF16))
    st_ref[...] = st * jnp.exp(b_end) + upd * hm
    return o


def _hgrn_scores_fast(ch, lvl, hm_b):
    d = ch["b"] - ch["b_mid"]
    a = _dot_nt((ch["q"] * jnp.exp(d)).astype(BF16), _block_diag(ch["k"] * jnp.exp(-d), hm_b))
    return jnp.where(lvl >= 0.0, a, 0.0)


def _hgrn_scores_levels(ch, w3, lvl, hm_b):
    c = CHUNK
    q, k = ch["q"], ch["k"]
    sums = _dot(w3, ch["g3"])
    b = sums[0:c]
    a = jnp.where(lvl == 0.0, _dot_nt(q.astype(BF16), _block_diag(k, hm_b)), 0.0)
    for l in range(1, LEVELS + 1):
        d = jnp.exp(-jnp.abs(b - sums[l * c:(l + 1) * c]))
        a = jnp.where(lvl == float(l), _dot_nt((q * d).astype(BF16), _block_diag(k * d, hm_b)), a)
    return a


def _hgrn_kernel(qf_ref, zf_ref, vf_ref, qb_ref, zb_ref, vb_ref, lbf_ref, lbb_ref,
                 w3f_ref, w3b_ref, lvf_ref, lvb_ref, hm_ref, of_ref, ob_ref, sf_ref, sb_ref):
    @pl.when(pl.program_id(1) == 0)
    def _():
        sf_ref[...] = jnp.zeros_like(sf_ref)
        sb_ref[...] = jnp.zeros_like(sb_ref)

    hm = hm_ref[...]
    hm_b = hm.astype(BF16)
    n = TILE // CHUNK
    fwd = dict(q=qf_ref, z=zf_ref, v=vf_ref, lb=lbf_ref, w3=w3f_ref, lvl=lvf_ref, o=of_ref, st=sf_ref,
               end=CHUNK - 1)
    bwd = dict(q=qb_ref, z=zb_ref, v=vb_ref, lb=lbb_ref, w3=w3b_ref, lvl=lvb_ref, o=ob_ref, st=sb_ref,
               end=0)
    order = [(d, ci if d is fwd else n - 1 - ci) for ci in range(n) for d in (fwd, bwd)]

    chunks = []
    spread = None
    for d, ci in order:
        rows = slice(ci * CHUNK, (ci + 1) * CHUNK)
        g, k = _hgrn_gates(d["z"][0, rows, :], d["lb"][...])
        g3 = _split3(g)
        sums = _dot(d["w3"][0:2 * CHUNK, :], g3)
        ch = dict(q=d["q"][0, rows, :], v=d["v"][0, rows, :], k=k, g3=g3,
                  b=sums[0:CHUNK], b_mid=sums[CHUNK:2 * CHUNK])
        dist = jnp.abs(ch["b"] - ch["b_mid"])
        spread = dist if spread is None else jnp.maximum(spread, dist)
        chunks.append(ch)
    fast = jnp.max(spread) < HGRN_FAST_RANGE

    @pl.when(fast)
    def _():
        for (d, ci), ch in zip(order, chunks):
            a = _hgrn_scores_fast(ch, d["lvl"][...], hm_b)
            d["o"][0, ci * CHUNK:(ci + 1) * CHUNK, :] = _hgrn_output(ch, a, hm, hm_b, d["st"], d["end"])

    @pl.when(jnp.logical_not(fast))
    def _():
        for (d, ci), ch in zip(order, chunks):
            a = _hgrn_scores_levels(ch, d["w3"][...], d["lvl"][...], hm_b)
            d["o"][0, ci * CHUNK:(ci + 1) * CHUNK, :] = _hgrn_output(ch, a, hm, hm_b, d["st"], d["end"])


def _hgrn(proj, lb_f, lb_b, tables):
    b, t, _ = proj.shape
    nt = t // TILE
    w3f, w3b, lvf, lvb, hm = tables

    def fwd(col):
        return pl.BlockSpec((1, TILE, GROUP),
                            lambda i, j: (i, jnp.where(j == 0, nt - 1, j - 1), col // GROUP))

    def bwd(col):
        return pl.BlockSpec((1, TILE, GROUP), lambda i, j: (i, nt - 1 - j, col // GROUP))

    def const(a):
        return pl.BlockSpec(a.shape, lambda i, j: (0,) * a.ndim)

    out_f = fwd(0)
    out_b = bwd(0)
    lb_f = lb_f.reshape(1, GROUP)
    lb_b = lb_b.reshape(1, GROUP)
    return pl.pallas_call(
        _hgrn_kernel,
        grid=(b, nt),
        in_specs=[fwd(COL_B_Q), fwd(COL_B_FF), fwd(COL_B_I), bwd(COL_B_Q), bwd(COL_B_FB), bwd(COL_B_I),
                  const(lb_f), const(lb_b), const(w3f), const(w3b), const(lvf), const(lvb), const(hm)],
        out_specs=[out_f, out_b],
        out_shape=[jax.ShapeDtypeStruct((b, t, GROUP), F32)] * 2,
        scratch_shapes=[pltpu.VMEM((GROUP, GROUP), F32), pltpu.VMEM((GROUP, GROUP), F32)],
        compiler_params=_cparams(2),
        name="hgrn2",
    )(proj, proj, proj, proj, proj, proj, lb_f, lb_b, w3f, w3b, lvf, lvb, hm)


def _outffn_kernel(x_ref, ma_ref, of_ref, ob_ref, g_ref, mc_ref, md_ref, mod_ref, hg_ref, n2_ref,
                   fn_ref, bd_ref, wo_ref, wi_ref, wd_ref, o_ref, act_ref, *, hidden, final):
    n_sub = x_ref.shape[1] // TILE
    o = of_ref[0] + ob_ref[0]
    g = g_ref[0]
    mb = (o * lax.rsqrt(_group_mean_sq(o, bd_ref[...]) + RMS_EPS) * hg_ref[...]) * (g * jax.nn.sigmoid(g))
    mix = jnp.concatenate([ma_ref[0], mb, mc_ref[0], md_ref[0]], axis=-1).astype(BF16)
    y = _dot(mix, wo_ref[...])
    xs, hs, gates = [], [], []
    for i in range(n_sub):
        m = _mod_rows(mod_ref, i, n_sub, not final)
        x = x_ref[0, i * TILE:(i + 1) * TILE, :] + m[2:3] * y[i * TILE:(i + 1) * TILE]
        ms = jnp.mean(x * x, axis=-1, keepdims=True)
        xs.append(x)
        hs.append((x * lax.rsqrt(ms + RMS_EPS) * (n2_ref[...] * (1.0 + m[4:5])) + m[3:4]).astype(BF16))
        gates.append(m[5:6])
    h = jnp.concatenate(hs, axis=0)
    step = 256
    for c in range(hidden // step):
        gate = _dot(h, wi_ref[:, c * step:(c + 1) * step])
        up = _dot(h, wi_ref[:, hidden + c * step:hidden + (c + 1) * step])
        act_ref[:, c * step:(c + 1) * step] = (gate * jax.nn.sigmoid(gate) * up).astype(BF16)
    acc = _dot(act_ref[...], wd_ref[...])
    for i in range(n_sub):
        x = xs[i] + gates[i] * acc[i * TILE:(i + 1) * TILE]
        if final:
            ms = jnp.mean(x * x, axis=-1, keepdims=True)
            x = x * lax.rsqrt(ms + RMS_EPS) * fn_ref[...]
        o_ref[0, i * TILE:(i + 1) * TILE, :] = x


def _out_ffn(xs, ma, o_f, o_b, proj, mc, md, modsel, hgrn_gain, norm2, final_norm, bd, w_out,
             w_ffn_in, w_ffn_out, n_rows, final):
    b, _, d = xs.shape
    hidden = w_ffn_out.shape[0]
    tm = _row_tile(n_rows) if not final else 2 * TILE

    def rows(width, col=0):
        return pl.BlockSpec((1, tm, width), lambda i, j: (i, j, col))

    hg = jnp.tile(hgrn_gain, GROUP // HEAD_DIM).reshape(1, GROUP)
    return pl.pallas_call(
        functools.partial(_outffn_kernel, hidden=hidden, final=final),
        grid=(b, n_rows // tm),
        in_specs=[rows(d), rows(GROUP), rows(GROUP), rows(GROUP), rows(GROUP, COL_B_G // GROUP),
                  rows(GROUP), rows(GROUP),
                  pl.BlockSpec((1, 2, 6, d), lambda i, j: (i, 0, 0, 0)),
                  _resident((1, GROUP)), _resident((1, d)), _resident((1, d)),
                  _resident((GROUP, GROUP)), _resident(w_out.shape), _resident(w_ffn_in.shape),
                  _resident(w_ffn_out.shape)],
        out_specs=rows(d),
        out_shape=jax.ShapeDtypeStruct((b, n_rows, d), F32),
        scratch_shapes=[pltpu.VMEM((tm, hidden), BF16)],
        compiler_params=_cparams(2),
        name="out_ffn",
    )(xs, ma, o_f, o_b, proj, mc, md, modsel, hg, norm2.reshape(1, d), final_norm.reshape(1, d), bd,
      w_out, w_ffn_in, w_ffn_out)


def _rope_tables(tc, s, dim, width):
    pos = jnp.arange(s)
    row = (pos // GRID_W).astype(F32)
    col = (pos % GRID_W).astype(F32)
    n_freq = dim // 4
    inv_freq = ROPE_THETA ** (-jnp.arange(n_freq, dtype=F32) / n_freq)
    lane = np.arange(width) % dim
    is_col = (lane // (dim // 2)) == 1
    second = ((lane % (dim // 2)) // n_freq) == 1
    freq = inv_freq[lane % n_freq]
    ang = jnp.where(is_col[None, :], col[:, None], row[:, None]) * freq[None, :]
    cos, sin = jnp.cos(ang), jnp.sin(ang)
    sin_a = jnp.where(second[None, :], sin, 0.0)
    sin_b = jnp.where(second[None, :], 0.0, -sin)
    ident = jnp.ones((tc, width), F32)
    zero = jnp.zeros((tc, width), F32)
    return (jnp.concatenate([cos, ident], axis=0), jnp.concatenate([sin_a, zero], axis=0),
            jnp.concatenate([sin_b, zero], axis=0))


def kernel(x, c, ctx, c_ctx, w_mod, b_mod, norm1, w_in, diff_lambda, diff_norm, hgrn_lb_logits,
           hgrn_norm, q_norm, k_norm, w_out, norm2, w_ffn_in, w_ffn_out, final_norm):
    b, s, d = x.shape
    tc = ctx.shape[1]
    depth = w_mod.shape[0]
    assert tc == TILE and s % TILE == 0 and s % GRID_W == 0 and d == 4 * GROUP

    rope_a = _rope_tables(tc, s, A_QK, GROUP)
    rope_c = _rope_tables(tc, s, HEAD_DIM, GROUP)
    dft = _dft_tables(tc, s)
    hgrn_tabs = _hgrn_tables()
    hgrn_tabs = tuple(jnp.asarray(a, BF16 if i < 2 else F32) for i, a in enumerate(hgrn_tabs))
    bd = jnp.asarray(np.kron(np.eye(GROUP // HEAD_DIM), np.full((HEAD_DIM, HEAD_DIM), 1.0 / HEAD_DIM)), BF16)

    p = jax.nn.softmax(hgrn_lb_logits.astype(F32), axis=1)
    lower = jnp.cumsum(p, axis=1) - p[:, :1]

    hp = np.arange(GROUP).reshape(4, HEAD_DIM)[[0, 2, 1, 3]].reshape(-1)
    w_in_b = w_in.astype(BF16)
    w_in_b = w_in_b.at[:, :, COL_C_Q:COL_C_Q + GROUP].set(w_in_b[:, :, COL_C_Q + hp])
    w_out_b = w_out.astype(BF16)
    w_out_b = w_out_b.at[:, 2 * GROUP:3 * GROUP, :].set(w_out_b[:, 2 * GROUP + hp, :])
    w_ffn_in_b = w_ffn_in.astype(BF16)
    w_ffn_out_b = w_ffn_out.astype(BF16)

    rows = -(-(b + 1) // 8) * 8
    cc = jnp.zeros((rows, d), F32).at[:b].set(c).at[b].set(c_ctx)
    mod = _modulation(cc, w_mod, b_mod).reshape(depth, rows, 6, d)
    modsel = jnp.stack([jnp.broadcast_to(mod[:, b:b + 1], (depth, b, 6, d)), mod[:, :b]], axis=2)

    xs = jnp.concatenate([x, ctx], axis=1)
    for l in range(depth):
        last = l == depth - 1
        lam_init = 0.8 - 0.6 * math.exp(-0.3 * l)
        proj = _in_proj(xs, modsel[l], norm1[l], w_in_b[l])
        m_a = _attn_a(proj, rope_a, diff_lambda[l], diff_norm[l], bd, tc, lam_init)
        o_f, o_b = _hgrn(proj, lower[0, l], lower[1, l], hgrn_tabs)
        m_c = _attn_c(proj, rope_c, q_norm[l], k_norm[l], bd, tc)
        m_d = _fourier(proj, *dft, tc)
        xs = _out_ffn(xs, m_a, o_f, o_b, proj, m_c, m_d, modsel[l], hgrn_norm[l], norm2[l], final_norm,
                      bd, w_out_b[l], w_ffn_in_b[l], w_ffn_out_b[l], s if last else s + tc, last)
    return xs
```

```python
import functools
import math

import numpy as np
import jax
import jax.numpy as jnp
from jax import lax
from jax.experimental import pallas as pl
from jax.experimental.pallas import tpu as pltpu

F32 = jnp.float32
BF16 = jnp.bfloat16

HEAD_DIM = 64
GRID_W = 64
ROPE_THETA = 10000.0
RMS_EPS = 1e-6
GROUP = 256
A_QK = 32
TILE = 256
CHUNK = 64
LEVELS = 6
LOG2E = 1.4426950408889634
VMEM_LIMIT = 56 * 1024 * 1024

COL_A_Q, COL_A_K, COL_A_V = 0, 256, 512
COL_B_Q, COL_B_FF, COL_B_FB, COL_B_I, COL_B_G = 768, 1024, 1280, 1536, 1792
COL_C_Q, COL_C_K, COL_C_V = 2048, 2304, 2432
COL_D_U = 2560


def _cparams(n_axes):
    return pltpu.CompilerParams(dimension_semantics=("arbitrary",) * n_axes,
                                vmem_limit_bytes=VMEM_LIMIT)


def _dot(a, b):
    return jnp.dot(a, b, preferred_element_type=F32)


def _dot_nt(a, b):
    return lax.dot_general(a, b, (((1,), (1,)), ((), ())), preferred_element_type=F32)


def _dot_tn(a, b):
    return lax.dot_general(a, b, (((0,), (0,)), ((), ())), preferred_element_type=F32)


def _group_mean_sq(y, bd):
    y2 = y * y
    hi = y2.astype(BF16)
    lo = (y2 - hi.astype(F32)).astype(BF16)
    return _dot(hi, bd) + _dot(lo, bd)


def _rope(x, cos, sin_a, sin_b, shift):
    w = x.shape[-1]
    return x * cos + pltpu.roll(x, shift, 1) * sin_a + pltpu.roll(x, w - shift, 1) * sin_b


def _mod_kernel(c_ref, w_ref, b_ref, o_ref):
    c = c_ref[...]
    a = c * jax.nn.sigmoid(c)
    o_ref[0] = _dot(a.astype(BF16), w_ref[0].astype(BF16)) + b_ref[0]


def _modulation(cc, w_mod, b_mod):
    depth, d, n = w_mod.shape
    r = cc.shape[0]
    tn = 1536
    return pl.pallas_call(
        _mod_kernel,
        grid=(depth, n // tn),
        in_specs=[pl.BlockSpec((r, d), lambda l, j: (0, 0)),
                  pl.BlockSpec((1, d, tn), lambda l, j: (l, 0, j)),
                  pl.BlockSpec((1, 1, tn), lambda l, j: (l, 0, j))],
        out_specs=pl.BlockSpec((1, r, tn), lambda l, j: (l, 0, j)),
        out_shape=jax.ShapeDtypeStruct((depth, r, n), F32),
        compiler_params=_cparams(2),
        name="modulation",
    )(cc, w_mod, b_mod.reshape(depth, 1, n))


def _mod_rows(mod_ref, sub, n_sub, with_ctx):
    m = mod_ref[0]
    if not (with_ctx and sub == n_sub - 1):
        return m[1]
    return jnp.where(pl.program_id(1) == pl.num_programs(1) - 1, m[0], m[1])


def _inproj_kernel(x_ref, mod_ref, n1_ref, w_ref, o_ref):
    n_sub = x_ref.shape[1] // TILE
    hs = []
    for i in range(n_sub):
        x = x_ref[0, i * TILE:(i + 1) * TILE, :]
        ms = jnp.mean(x * x, axis=-1, keepdims=True)
        y = x * lax.rsqrt(ms + RMS_EPS) * n1_ref[...]
        m = _mod_rows(mod_ref, i, n_sub, True)
        hs.append((y * (1.0 + m[1:2]) + m[0:1]).astype(BF16))
    o_ref[0] = _dot(jnp.concatenate(hs, axis=0), w_ref[...])


def _row_tile(t):
    return next(tm for tm in (3 * TILE, 2 * TILE, TILE) if t % tm == 0)


def _resident(shape):
    return pl.BlockSpec(shape, lambda i, j: (0,) * len(shape), pipeline_mode=pl.Buffered(1))


def _in_proj(xs, modsel, norm1, w_in):
    b, t, d = xs.shape
    n = w_in.shape[1]
    tm = _row_tile(t)
    return pl.pallas_call(
        _inproj_kernel,
        grid=(b, t // tm),
        in_specs=[pl.BlockSpec((1, tm, d), lambda i, j: (i, j, 0)),
                  pl.BlockSpec((1, 2, 6, d), lambda i, j: (i, 0, 0, 0)),
                  _resident((1, d)), _resident((d, n))],
        out_specs=pl.BlockSpec((1, tm, n), lambda i, j: (i, j, 0)),
        out_shape=jax.ShapeDtypeStruct((b, t, n), F32),
        compiler_params=_cparams(2),
        name="in_proj",
    )(xs, modsel, norm1.reshape(1, d), w_in)


V_ROWS = HEAD_DIM + 16


KEY_TILE = 256


def _attend_t(groups, s_ref, lo, nk):
    n_kt = nk // KEY_TILE
    n_g = len(groups)
    maxes = [None] * n_g
    accs = [None] * n_g

    def keys(kt):
        return slice(lo + kt * KEY_TILE, lo + (kt + 1) * KEY_TILE)

    def score_tile(g, kt):
        kr_ref, wq, _, _ = groups[g]
        s = _dot_nt(kr_ref[keys(kt), :], wq)
        s_ref[g % 2, kt * KEY_TILE:(kt + 1) * KEY_TILE, :] = s
        m = jnp.max(s, axis=0, keepdims=True)
        maxes[g] = m if kt == 0 else jnp.maximum(maxes[g], m)

    def value_tile(g, kt, p):
        _, _, vt_ref, head = groups[g]
        pv = _dot(vt_ref[head, :, keys(kt)], p)
        accs[g] = pv if kt == 0 else accs[g] + pv

    for kt in range(n_kt):
        score_tile(0, kt)
    for g in range(n_g):
        p_prev = None
        for kt in range(n_kt):
            if g + 1 < n_g:
                score_tile(g + 1, kt)
            p = jnp.exp2(s_ref[g % 2, kt * KEY_TILE:(kt + 1) * KEY_TILE, :] - maxes[g]).astype(BF16)
            if kt > 0:
                value_tile(g, kt - 1, p_prev)
            p_prev = p
        value_tile(g, n_kt - 1, p_prev)
    return [a[0:HEAD_DIM] * (1.0 / a[HEAD_DIM:HEAD_DIM + 1]) for a in accs]


def _store_values_t(vt_ref, v):
    v_t = v.T
    for h in range(vt_ref.shape[0]):
        vt_ref[h, 0:HEAD_DIM, :] = v_t[h * HEAD_DIM:(h + 1) * HEAD_DIM, :].astype(BF16)
        vt_ref[h, HEAD_DIM:V_ROWS, :] = jnp.ones((V_ROWS - HEAD_DIM, v.shape[0]), BF16)


def _attn_kernel(aq_ref, ak_ref, av_ref, cq_ref, ck_ref, cv_ref,
                 a_cos_q, a_sa_q, a_sb_q, a_cos_k, a_sa_k, a_sb_k,
                 c_cos_q, c_sa_q, c_sb_q, c_cos_k, c_sa_k, c_sb_k,
                 lp_ref, gain_ref, qg_ref, kg_ref, bd_ref, oa_ref, oc_ref,
                 akr_ref, avt_ref, ckr_ref, cvt_ref, s_ref, *, tc, lam_init):
    t = pl.program_id(1)
    n_tok = akr_ref.shape[0]
    half = GROUP // 2
    bd = bd_ref[...]

    @pl.when(t == 0)
    def _():
        kr = _rope(ak_ref[0], a_cos_k[...], a_sa_k[...], a_sb_k[...], A_QK // 4)
        akr_ref[...] = kr.astype(BF16)
        _store_values_t(avt_ref, av_ref[0])
        k = ck_ref[0]
        kn = k * lax.rsqrt(_group_mean_sq(k, bd[0:half, 0:half]) + RMS_EPS) * kg_ref[...]
        ckr_ref[...] = _rope(kn, c_cos_k[...], c_sa_k[...], c_sb_k[...], HEAD_DIM // 4).astype(BF16)
        _store_values_t(cvt_ref, cv_ref[0])

    qa = _rope(aq_ref[0], a_cos_q[...], a_sa_q[...], a_sb_q[...], A_QK // 4) * (A_QK ** -0.5 * LOG2E)
    q = cq_ref[0]
    qn = q * lax.rsqrt(_group_mean_sq(q, bd) + RMS_EPS) * qg_ref[...]
    qc = _rope(qn, c_cos_q[...], c_sa_q[...], c_sb_q[...], HEAD_DIM // 4) * (HEAD_DIM ** -0.5 * LOG2E)
    lp = lp_ref[...]
    lam = (jnp.exp(jnp.sum(lp[0:1] * lp[1:2], axis=-1, keepdims=True))
           - jnp.exp(jnp.sum(lp[2:3] * lp[3:4], axis=-1, keepdims=True)) + lam_init)
    lane = lax.broadcasted_iota(jnp.int32, (1, GROUP), 1)
    lane_h = lax.broadcasted_iota(jnp.int32, (1, half), 1)
    n_heads = GROUP // HEAD_DIM

    def attend(lo, nk):
        groups = [(akr_ref, jnp.concatenate([jnp.where(lane // A_QK == 2 * h + c, qa, 0.0).astype(BF16)
                                             for c in range(2)], axis=0), avt_ref, h)
                  for h in range(n_heads)]
        groups += [(ckr_ref, jnp.concatenate(
            [jnp.where(lane_h // HEAD_DIM == g, qc[:, qb * half:(qb + 1) * half], 0.0).astype(BF16)
             for qb in range(2)], axis=0), cvt_ref, g) for g in range(2)]
        outs = _attend_t(groups, s_ref, lo, nk)
        heads = [o[:, 0:TILE] - lam * o[:, TILE:2 * TILE] for o in outs[0:n_heads]]
        acc = jnp.concatenate(heads, axis=0).T
        y = acc * lax.rsqrt(_group_mean_sq(acc, bd) + RMS_EPS)
        oa_ref[0] = y * gain_ref[...] * (1.0 - lam_init)
        heads = [outs[n_heads + g][:, qb * TILE:(qb + 1) * TILE] for qb in range(2) for g in range(2)]
        oc_ref[0] = jnp.concatenate(heads, axis=0).T

    @pl.when(t == pl.num_programs(1) - 1)
    def _():
        attend(n_tok - tc, tc)

    @pl.when(t < pl.num_programs(1) - 1)
    def _():
        attend(0, n_tok)


def _attention(proj, rope_a, rope_c, lam_params, sub_gain, q_gain, k_gain, bd, tc, lam_init):
    b, t, _ = proj.shape
    half = GROUP // 2
    gain = jnp.tile(sub_gain, GROUP // HEAD_DIM).reshape(1, GROUP)
    qg = jnp.tile(q_gain, GROUP // HEAD_DIM).reshape(1, GROUP)
    kg = jnp.tile(k_gain, half // HEAD_DIM).reshape(1, half)

    def rows(width, col):
        return pl.BlockSpec((1, TILE, width), lambda i, j: (i, j, col // width))

    def sample(width, col):
        return pl.BlockSpec((1, t, width), lambda i, j: (i, 0, col // width))

    tile_spec = pl.BlockSpec((TILE, GROUP), lambda i, j: (j, 0))
    out_spec = pl.BlockSpec((1, TILE, GROUP), lambda i, j: (i, j, 0))
    return pl.pallas_call(
        functools.partial(_attn_kernel, tc=tc, lam_init=lam_init),
        grid=(b, t // TILE),
        in_specs=[rows(GROUP, COL_A_Q), sample(GROUP, COL_A_K), sample(GROUP, COL_A_V),
                  rows(GROUP, COL_C_Q), sample(half, COL_C_K), sample(half, COL_C_V),
                  tile_spec, tile_spec, tile_spec,
                  _resident((t, GROUP)), _resident((t, GROUP)), _resident((t, GROUP)),
                  tile_spec, tile_spec, tile_spec,
                  _resident((t, half)), _resident((t, half)), _resident((t, half)),
                  _resident(lam_params.shape), _resident((1, GROUP)), _resident((1, GROUP)),
                  _resident((1, half)), _resident((GROUP, GROUP))],
        out_specs=[out_spec, out_spec],
        out_shape=[jax.ShapeDtypeStruct((b, t, GROUP), F32)] * 2,
        scratch_shapes=[pltpu.VMEM((t, GROUP), BF16),
                        pltpu.VMEM((GROUP // HEAD_DIM, V_ROWS, t), BF16),
                        pltpu.VMEM((t, half), BF16),
                        pltpu.VMEM((half // HEAD_DIM, V_ROWS, t), BF16),
                        pltpu.VMEM((2, t, 2 * TILE), F32)],
        compiler_params=_cparams(2),
        name="attention",
    )(proj, proj, proj, proj, proj, proj, *rope_a, *rope_a, *rope_c, *rope_c,
      lam_params, gain, qg, kg, bd)


def _fourier_stage_kernel(ux_ref, uc_ref, c64_ref, dctx_ref, w_ref, o_ref):
    s = ux_ref.shape[1]
    tc = uc_ref.shape[1]
    w = _dot(ux_ref[0].astype(BF16), c64_ref[...])
    w_ref[0:s, :] = w[:, 0:GROUP].astype(BF16)
    w_ref[s:2 * s, :] = w[:, GROUP:2 * GROUP].astype(BF16)
    wc = _dot(uc_ref[0].astype(BF16), c64_ref[...])
    wc = jnp.concatenate([wc[:, 0:GROUP], wc[:, GROUP:2 * GROUP]], axis=0).astype(BF16)
    o_ref[0] = _dot(dctx_ref[...], wc) * (tc ** -0.5)


def _fourier_dft_kernel(d_ref, w_ref, alias_ref, o_ref):
    del alias_ref
    o_ref[0] = _dot(d_ref[...], w_ref[...]) * ((d_ref.shape[1] // 2) ** -0.5)


def _fourier(proj, c64, dctx, dx, tc):
    b, t, _ = proj.shape
    s = t - tc
    col = COL_D_U // GROUP
    w_all, o_ctx = pl.pallas_call(
        _fourier_stage_kernel,
        grid=(b,),
        in_specs=[pl.BlockSpec((1, s, GROUP), lambda i: (i, 0, col)),
                  pl.BlockSpec((1, tc, GROUP), lambda i: (i, s // tc, col)),
                  pl.BlockSpec((GROUP, 2 * GROUP), lambda i: (0, 0)),
                  pl.BlockSpec((tc, 2 * tc), lambda i: (0, 0))],
        out_specs=[pl.BlockSpec((2 * s, GROUP), lambda i: (0, i)),
                   pl.BlockSpec((1, tc, GROUP), lambda i: (i, s // tc, 0))],
        out_shape=[jax.ShapeDtypeStruct((2 * s, b * GROUP), BF16),
                   jax.ShapeDtypeStruct((b, t, GROUP), F32)],
        compiler_params=_cparams(1),
        name="fourier_stage",
    )(proj, proj, c64, dctx)
    tm = 2 * TILE
    return pl.pallas_call(
        _fourier_dft_kernel,
        grid=(s // tm, b),
        in_specs=[pl.BlockSpec((tm, 2 * s), lambda i, j: (i, 0)),
                  pl.BlockSpec((2 * s, GROUP), lambda i, j: (0, j)),
                  pl.BlockSpec(memory_space=pl.ANY)],
        out_specs=pl.BlockSpec((1, tm, GROUP), lambda i, j: (j, i, 0)),
        out_shape=jax.ShapeDtypeStruct((b, t, GROUP), F32),
        input_output_aliases={2: 0},
        compiler_params=_cparams(2),
        name="fourier_dft",
    )(dx, w_all, o_ctx)


def _dft_tables(tc, s):
    def cs_direct(rows, n, period):
        j = jnp.arange(rows, dtype=jnp.int32)
        k = jnp.arange(n, dtype=jnp.int32)
        ang = ((j[:, None] * k[None, :]) % period).astype(F32) * (2.0 * math.pi / period)
        return jnp.cos(ang), jnp.sin(ang)

    def cs(n):
        ca, sa = cs_direct(n // GRID_W, n, n // GRID_W)
        cb, sb = cs_direct(GRID_W, n, n)
        cos = ca[:, None, :] * cb[None, :, :] - sa[:, None, :] * sb[None, :, :]
        sin = sa[:, None, :] * cb[None, :, :] + ca[:, None, :] * sb[None, :, :]
        return cos.reshape(n, n), sin.reshape(n, n)

    c64, s64 = cs_direct(HEAD_DIM, HEAD_DIM, HEAD_DIM)
    eye = jnp.eye(GROUP // HEAD_DIM, dtype=F32)
    c64 = jnp.concatenate([jnp.kron(eye, c64), jnp.kron(eye, s64)], axis=1) * (HEAD_DIM ** -0.5)
    cc, sc = cs(tc)
    cx, sx = cs(s)
    return (c64.astype(BF16), jnp.concatenate([cc, -sc], axis=1).astype(BF16),
            jnp.concatenate([cx, -sx], axis=1).astype(BF16))


def _hgrn_tables():
    c = CHUNK
    idx = np.arange(c)
    tri = (idx[None, :] <= idx[:, None]).astype(np.float32)
    lvl = np.full((c, c), -1.0, np.float32)
    lvl[idx, idx] = 0.0
    rows = [tri]
    for l in range(1, LEVELS + 1):
        w = c >> (l - 1)
        blk, pos = idx // w, idx % w
        ref = blk * w + w // 2 - 1
        rows.append(tri[ref])
        m = (blk[:, None] == blk[None, :]) & (pos[:, None] >= w // 2) & (pos[None, :] < w // 2)
        lvl[m] = float(l)
    w_f = np.concatenate(rows, axis=0)
    w_b = w_f.reshape(LEVELS + 1, c, c)[:, ::-1, ::-1].reshape(-1, c)
    heads = GROUP // HEAD_DIM
    head_mask = np.kron(np.eye(heads, dtype=np.float32), np.ones((c, HEAD_DIM), np.float32))
    return (np.tile(w_f, (1, 3)), np.tile(w_b, (1, 3)),
            np.tile(lvl, (1, heads)), np.tile(lvl.T, (1, heads)), head_mask)


HGRN_FAST_RANGE = 75.0


def _hgrn_gates(z, lb):
    e = jnp.exp(-jnp.abs(z))
    r = 1.0 / (1.0 + e)
    sig_pos = jnp.where(z >= 0, r, e * r)
    sig_neg = jnp.where(z >= 0, e * r, r)
    return jnp.log(lb + (1.0 - lb) * sig_pos), (1.0 - lb) * sig_neg


def _split3(g):
    g_hi = g.astype(BF16)
    r1 = g - g_hi.astype(F32)
    g_mid = r1.astype(BF16)
    g_lo = (r1 - g_mid.astype(F32)).astype(BF16)
    return jnp.concatenate([g_hi, g_mid, g_lo], axis=0)


def _block_diag(x, hm_b):
    return jnp.concatenate([x.astype(BF16)] * (GROUP // HEAD_DIM), axis=0) * hm_b


def _hgrn_output(ch, a, hm, hm_b, st_ref, end_row):
    q, k, v, b = ch["q"], ch["k"], ch["v"], ch["b"]
    b_end = b[end_row:end_row + 1]
    st = st_ref[...]
    o = _dot(a.astype(BF16), _block_diag(v, hm_b))
    o = o + _dot_nt((q * jnp.exp(b)).astype(BF16), st.astype(BF16))
    upd = _dot_tn(v.astype(BF16), (k * jnp.exp(b_end - b)).astype(BF16))
    st_ref[...] = st * jnp.exp(b_end) + upd * hm
    return o


def _hgrn_scores_fast(ch, lvl, hm_b):
    d = ch["b"] - ch["b_mid"]
    a = _dot_nt((ch["q"] * jnp.exp(d)).astype(BF16), _block_diag(ch["k"] * jnp.exp(-d), hm_b))
    return jnp.where(lvl >= 0.0, a, 0.0)


def _hgrn_scores_levels(ch, w3, lvl, hm_b):
    c = CHUNK
    q, k = ch["q"], ch["k"]
    sums = _dot(w3, ch["g3"])
    b = sums[0:c]
    a = jnp.where(lvl == 0.0, _dot_nt(q.astype(BF16), _block_diag(k, hm_b)), 0.0)
    for l in range(1, LEVELS + 1):
        d = jnp.exp(-jnp.abs(b - sums[l * c:(l + 1) * c]))
        a = jnp.where(lvl == float(l), _dot_nt((q * d).astype(BF16), _block_diag(k * d, hm_b)), a)
    return a


def _hgrn_kernel(qf_ref, zf_ref, vf_ref, qb_ref, zb_ref, vb_ref, lbf_ref, lbb_ref,
                 w3f_ref, w3b_ref, lvf_ref, lvb_ref, hm_ref, of_ref, ob_ref, sf_ref, sb_ref):
    @pl.when(pl.program_id(1) == 0)
    def _():
        sf_ref[...] = jnp.zeros_like(sf_ref)
        sb_ref[...] = jnp.zeros_like(sb_ref)

    hm = hm_ref[...]
    hm_b = hm.astype(BF16)
    n = TILE // CHUNK
    fwd = dict(q=qf_ref, z=zf_ref, v=vf_ref, lb=lbf_ref, w3=w3f_ref, lvl=lvf_ref, o=of_ref, st=sf_ref,
               end=CHUNK - 1)
    bwd = dict(q=qb_ref, z=zb_ref, v=vb_ref, lb=lbb_ref, w3=w3b_ref, lvl=lvb_ref, o=ob_ref, st=sb_ref,
               end=0)
    order = [(d, ci if d is fwd else n - 1 - ci) for ci in range(n) for d in (fwd, bwd)]

    chunks = []
    spread = None
    for d, ci in order:
        rows = slice(ci * CHUNK, (ci + 1) * CHUNK)
        g, k = _hgrn_gates(d["z"][0, rows, :], d["lb"][...])
        g3 = _split3(g)
        sums = _dot(d["w3"][0:2 * CHUNK, :], g3)
        ch = dict(q=d["q"][0, rows, :], v=d["v"][0, rows, :], k=k, g3=g3,
                  b=sums[0:CHUNK], b_mid=sums[CHUNK:2 * CHUNK])
        dist = jnp.abs(ch["b"] - ch["b_mid"])
        spread = dist if spread is None else jnp.maximum(spread, dist)
        chunks.append(ch)
    fast = jnp.max(spread) < HGRN_FAST_RANGE

    @pl.when(fast)
    def _():
        for (d, ci), ch in zip(order, chunks):
            a = _hgrn_scores_fast(ch, d["lvl"][...], hm_b)
            d["o"][0, ci * CHUNK:(ci + 1) * CHUNK, :] = _hgrn_output(ch, a, hm, hm_b, d["st"], d["end"])

    @pl.when(jnp.logical_not(fast))
    def _():
        for (d, ci), ch in zip(order, chunks):
            a = _hgrn_scores_levels(ch, d["w3"][...], d["lvl"][...], hm_b)
            d["o"][0, ci * CHUNK:(ci + 1) * CHUNK, :] = _hgrn_output(ch, a, hm, hm_b, d["st"], d["end"])


def _hgrn(proj, lb_f, lb_b, tables):
    b, t, _ = proj.shape
    nt = t // TILE
    w3f, w3b, lvf, lvb, hm = tables

    def fwd(col):
        return pl.BlockSpec((1, TILE, GROUP),
                            lambda i, j: (i, jnp.where(j == 0, nt - 1, j - 1), col // GROUP))

    def bwd(col):
        return pl.BlockSpec((1, TILE, GROUP), lambda i, j: (i, nt - 1 - j, col // GROUP))

    def const(a):
        return pl.BlockSpec(a.shape, lambda i, j: (0,) * a.ndim)

    out_f = fwd(0)
    out_b = bwd(0)
    lb_f = lb_f.reshape(1, GROUP)
    lb_b = lb_b.reshape(1, GROUP)
    return pl.pallas_call(
        _hgrn_kernel,
        grid=(b, nt),
        in_specs=[fwd(COL_B_Q), fwd(COL_B_FF), fwd(COL_B_I), bwd(COL_B_Q), bwd(COL_B_FB), bwd(COL_B_I),
                  const(lb_f), const(lb_b), const(w3f), const(w3b), const(lvf), const(lvb), const(hm)],
        out_specs=[out_f, out_b],
        out_shape=[jax.ShapeDtypeStruct((b, t, GROUP), F32)] * 2,
        scratch_shapes=[pltpu.VMEM((GROUP, GROUP), F32), pltpu.VMEM((GROUP, GROUP), F32)],
        compiler_params=_cparams(2),
        name="hgrn2",
    )(proj, proj, proj, proj, proj, proj, lb_f, lb_b, w3f, w3b, lvf, lvb, hm)


def _outffn_kernel(x_ref, ma_ref, of_ref, ob_ref, g_ref, mc_ref, md_ref, mod_ref, hg_ref, n2_ref,
                   fn_ref, bd_ref, wo_ref, wi_ref, wd_ref, o_ref, act_ref, *, hidden, final):
    n_sub = x_ref.shape[1] // TILE
    o = of_ref[0] + ob_ref[0]
    g = g_ref[0]
    mb = (o * lax.rsqrt(_group_mean_sq(o, bd_ref[...]) + RMS_EPS) * hg_ref[...]) * (g * jax.nn.sigmoid(g))
    mix = jnp.concatenate([ma_ref[0], mb, mc_ref[0], md_ref[0]], axis=-1).astype(BF16)
    y = _dot(mix, wo_ref[...])
    xs, hs, gates = [], [], []
    for i in range(n_sub):
        m = _mod_rows(mod_ref, i, n_sub, not final)
        x = x_ref[0, i * TILE:(i + 1) * TILE, :] + m[2:3] * y[i * TILE:(i + 1) * TILE]
        ms = jnp.mean(x * x, axis=-1, keepdims=True)
        xs.append(x)
        hs.append((x * lax.rsqrt(ms + RMS_EPS) * (n2_ref[...] * (1.0 + m[4:5])) + m[3:4]).astype(BF16))
        gates.append(m[5:6])
    h = jnp.concatenate(hs, axis=0)
    step = 256
    for c in range(hidden // step):
        gate = _dot(h, wi_ref[:, c * step:(c + 1) * step])
        up = _dot(h, wi_ref[:, hidden + c * step:hidden + (c + 1) * step])
        act_ref[:, c * step:(c + 1) * step] = (gate * jax.nn.sigmoid(gate) * up).astype(BF16)
    acc = _dot(act_ref[...], wd_ref[...])
    for i in range(n_sub):
        x = xs[i] + gates[i] * acc[i * TILE:(i + 1) * TILE]
        if final:
            ms = jnp.mean(x * x, axis=-1, keepdims=True)
            x = x * lax.rsqrt(ms + RMS_EPS) * fn_ref[...]
        o_ref[0, i * TILE:(i + 1) * TILE, :] = x


def _out_ffn(xs, ma, o_f, o_b, proj, mc, md, modsel, hgrn_gain, norm2, final_norm, bd, w_out,
             w_ffn_in, w_ffn_out, n_rows, final):
    b, _, d = xs.shape
    hidden = w_ffn_out.shape[0]
    tm = _row_tile(n_rows) if not final else 2 * TILE

    def rows(width, col=0):
        return pl.BlockSpec((1, tm, width), lambda i, j: (i, j, col))

    hg = jnp.tile(hgrn_gain, GROUP // HEAD_DIM).reshape(1, GROUP)
    return pl.pallas_call(
        functools.partial(_outffn_kernel, hidden=hidden, final=final),
        grid=(b, n_rows // tm),
        in_specs=[rows(d), rows(GROUP), rows(GROUP), rows(GROUP), rows(GROUP, COL_B_G // GROUP),
                  rows(GROUP), rows(GROUP),
                  pl.BlockSpec((1, 2, 6, d), lambda i, j: (i, 0, 0, 0)),
                  _resident((1, GROUP)), _resident((1, d)), _resident((1, d)),
                  _resident((GROUP, GROUP)), _resident(w_out.shape), _resident(w_ffn_in.shape),
                  _resident(w_ffn_out.shape)],
        out_specs=rows(d),
        out_shape=jax.ShapeDtypeStruct((b, n_rows, d), F32),
        scratch_shapes=[pltpu.VMEM((tm, hidden), BF16)],
        compiler_params=_cparams(2),
        name="out_ffn",
    )(xs, ma, o_f, o_b, proj, mc, md, modsel, hg, norm2.reshape(1, d), final_norm.reshape(1, d), bd,
      w_out, w_ffn_in, w_ffn_out)


def _rope_tables(tc, s, dim, width):
    pos = jnp.arange(s)
    row = (pos // GRID_W).astype(F32)
    col = (pos % GRID_W).astype(F32)
    n_freq = dim // 4
    inv_freq = ROPE_THETA ** (-jnp.arange(n_freq, dtype=F32) / n_freq)
    lane = np.arange(width) % dim
    is_col = (lane // (dim // 2)) == 1
    second = ((lane % (dim // 2)) // n_freq) == 1
    freq = inv_freq[lane % n_freq]
    ang = jnp.where(is_col[None, :], col[:, None], row[:, None]) * freq[None, :]
    cos, sin = jnp.cos(ang), jnp.sin(ang)
    sin_a = jnp.where(second[None, :], sin, 0.0)
    sin_b = jnp.where(second[None, :], 0.0, -sin)
    ident = jnp.ones((tc, width), F32)
    zero = jnp.zeros((tc, width), F32)
    return (jnp.concatenate([cos, ident], axis=0), jnp.concatenate([sin_a, zero], axis=0),
            jnp.concatenate([sin_b, zero], axis=0))


def kernel(x, c, ctx, c_ctx, w_mod, b_mod, norm1, w_in, diff_lambda, diff_norm, hgrn_lb_logits,
           hgrn_norm, q_norm, k_norm, w_out, norm2, w_ffn_in, w_ffn_out, final_norm):
    b, s, d = x.shape
    tc = ctx.shape[1]
    depth = w_mod.shape[0]
    assert tc == TILE and s % TILE == 0 and s % GRID_W == 0 and d == 4 * GROUP

    rope_a = _rope_tables(tc, s, A_QK, GROUP)
    rope_c = _rope_tables(tc, s, HEAD_DIM, GROUP)
    dft = _dft_tables(tc, s)
    hgrn_tabs = _hgrn_tables()
    hgrn_tabs = tuple(jnp.asarray(a, BF16 if i < 2 else F32) for i, a in enumerate(hgrn_tabs))
    bd = jnp.asarray(np.kron(np.eye(GROUP // HEAD_DIM), np.full((HEAD_DIM, HEAD_DIM), 1.0 / HEAD_DIM)), BF16)

    p = jax.nn.softmax(hgrn_lb_logits.astype(F32), axis=1)
    lower = jnp.cumsum(p, axis=1) - p[:, :1]

    def swap_heads(w, axis, start):
        cuts = [0, start + HEAD_DIM, start + 2 * HEAD_DIM, start + 3 * HEAD_DIM, w.shape[axis]]
        parts = [lax.slice_in_dim(w, cuts[i], cuts[i + 1], axis=axis) for i in range(4)]
        return jnp.concatenate([parts[0], parts[2], parts[1], parts[3]], axis=axis).astype(BF16)

    w_in_b = swap_heads(w_in, 2, COL_C_Q)
    w_out_b = swap_heads(w_out, 1, 2 * GROUP)
    w_ffn_in_b = w_ffn_in.astype(BF16)
    w_ffn_out_b = w_ffn_out.astype(BF16)

    rows = -(-(b + 1) // 8) * 8
    cc = jnp.zeros((rows, d), F32).at[:b].set(c).at[b].set(c_ctx)
    mod = _modulation(cc, w_mod, b_mod).reshape(depth, rows, 6, d)
    modsel = jnp.stack([jnp.broadcast_to(mod[:, b:b + 1], (depth, b, 6, d)), mod[:, :b]], axis=2)

    xs = jnp.concatenate([x, ctx], axis=1)
    for l in range(depth):
        last = l == depth - 1
        lam_init = 0.8 - 0.6 * math.exp(-0.3 * l)
        proj = _in_proj(xs, modsel[l], norm1[l], w_in_b[l])
        m_a, m_c = _attention(proj, rope_a, rope_c, diff_lambda[l], diff_norm[l], q_norm[l], k_norm[l],
                              bd, tc, lam_init)
        o_f, o_b = _hgrn(proj, lower[0, l], lower[1, l], hgrn_tabs)
        m_d = _fourier(proj, *dft, tc)
        xs = _out_ffn(xs, m_a, o_f, o_b, proj, m_c, m_d, modsel[l], hgrn_norm[l], norm2[l], final_norm,
                      bd, w_out_b[l], w_ffn_in_b[l], w_ffn_out_b[l], s if last else s + tc, last)
    return xs
```

```python
import functools
import math

import numpy as np
import jax
import jax.numpy as jnp
from jax import lax
from jax.experimental import pallas as pl
from jax.experimental.pallas import tpu as pltpu

F32 = jnp.float32
BF16 = jnp.bfloat16

HEAD_DIM = 64
GRID_W = 64
ROPE_THETA = 10000.0
RMS_EPS = 1e-6
GROUP = 256
A_QK = 32
TILE = 256
CHUNK = 64
LEVELS = 6
LOG2E = 1.4426950408889634
VMEM_LIMIT = 56 * 1024 * 1024

COL_A_Q, COL_A_K, COL_A_V = 0, 256, 512
COL_B_Q, COL_B_FF, COL_B_FB, COL_B_I, COL_B_G = 768, 1024, 1280, 1536, 1792
COL_C_Q, COL_C_K, COL_C_V = 2048, 2304, 2432
COL_D_U = 2560


def _cparams(n_axes):
    return pltpu.CompilerParams(dimension_semantics=("arbitrary",) * n_axes,
                                vmem_limit_bytes=VMEM_LIMIT)


def _dot(a, b):
    return jnp.dot(a, b, preferred_element_type=F32)


def _dot_nt(a, b):
    return lax.dot_general(a, b, (((1,), (1,)), ((), ())), preferred_element_type=F32)


def _dot_tn(a, b):
    return lax.dot_general(a, b, (((0,), (0,)), ((), ())), preferred_element_type=F32)


def _group_mean_sq(y, bd):
    y2 = y * y
    hi = y2.astype(BF16)
    lo = (y2 - hi.astype(F32)).astype(BF16)
    return _dot(hi, bd) + _dot(lo, bd)


def _rope(x, cos, sin_a, sin_b, shift):
    w = x.shape[-1]
    return x * cos + pltpu.roll(x, shift, 1) * sin_a + pltpu.roll(x, w - shift, 1) * sin_b


def _mod_kernel(c_ref, w_ref, b_ref, o_ref):
    c = c_ref[...]
    a = c * jax.nn.sigmoid(c)
    o_ref[0] = _dot(a.astype(BF16), w_ref[0].astype(BF16)) + b_ref[0]


def _modulation(cc, w_mod, b_mod):
    depth, d, n = w_mod.shape
    r = cc.shape[0]
    tn = 1536
    return pl.pallas_call(
        _mod_kernel,
        grid=(depth, n // tn),
        in_specs=[pl.BlockSpec((r, d), lambda l, j: (0, 0)),
                  pl.BlockSpec((1, d, tn), lambda l, j: (l, 0, j)),
                  pl.BlockSpec((1, 1, tn), lambda l, j: (l, 0, j))],
        out_specs=pl.BlockSpec((1, r, tn), lambda l, j: (l, 0, j)),
        out_shape=jax.ShapeDtypeStruct((depth, r, n), F32),
        compiler_params=_cparams(2),
        name="modulation",
    )(cc, w_mod, b_mod.reshape(depth, 1, n))


def _mod_rows(mod_ref, sub, n_sub, with_ctx):
    m = mod_ref[0]
    if not (with_ctx and sub == n_sub - 1):
        return m[1]
    return jnp.where(pl.program_id(1) == pl.num_programs(1) - 1, m[0], m[1])


def _inproj_kernel(x_ref, mod_ref, n1_ref, w_ref, o_ref):
    n_sub = x_ref.shape[1] // TILE
    hs = []
    for i in range(n_sub):
        x = x_ref[0, i * TILE:(i + 1) * TILE, :]
        ms = jnp.mean(x * x, axis=-1, keepdims=True)
        y = x * lax.rsqrt(ms + RMS_EPS) * n1_ref[...]
        m = _mod_rows(mod_ref, i, n_sub, True)
        hs.append((y * (1.0 + m[1:2]) + m[0:1]).astype(BF16))
    o_ref[0] = _dot(jnp.concatenate(hs, axis=0), w_ref[...])


def _row_tile(t):
    return next(tm for tm in (3 * TILE, 2 * TILE, TILE) if t % tm == 0)


def _resident(shape):
    return pl.BlockSpec(shape, lambda i, j: (0,) * len(shape), pipeline_mode=pl.Buffered(1))


def _in_proj(xs, modsel, norm1, w_in):
    b, t, d = xs.shape
    n = w_in.shape[1]
    tm = _row_tile(t)
    return pl.pallas_call(
        _inproj_kernel,
        grid=(b, t // tm),
        in_specs=[pl.BlockSpec((1, tm, d), lambda i, j: (i, j, 0)),
                  pl.BlockSpec((1, 2, 6, d), lambda i, j: (i, 0, 0, 0)),
                  _resident((1, d)), _resident((d, n))],
        out_specs=pl.BlockSpec((1, tm, n), lambda i, j: (i, j, 0)),
        out_shape=jax.ShapeDtypeStruct((b, t, n), F32),
        compiler_params=_cparams(2),
        name="in_proj",
    )(xs, modsel, norm1.reshape(1, d), w_in)


V_ROWS = HEAD_DIM + 16


KEY_TILES = (768, 256)


def _attend_t(groups, s_ref, lo, nk):
    key_tile = next(c for c in KEY_TILES if nk % c == 0)
    n_kt = nk // key_tile
    n_g = len(groups)
    maxes = [None] * n_g
    accs = [None] * n_g

    def keys(kt):
        return slice(lo + kt * key_tile, lo + (kt + 1) * key_tile)

    def score_tile(g, kt):
        kr_ref, wq, _, _ = groups[g]
        s = _dot_nt(kr_ref[keys(kt), :], wq)
        s_ref[g % 2, kt * key_tile:(kt + 1) * key_tile, :] = s
        m = jnp.max(s, axis=0, keepdims=True)
        maxes[g] = m if kt == 0 else jnp.maximum(maxes[g], m)

    def value_tile(g, kt, p):
        _, _, vt_ref, head = groups[g]
        pv = _dot(vt_ref[head, :, keys(kt)], p)
        accs[g] = pv if kt == 0 else accs[g] + pv

    for kt in range(n_kt):
        score_tile(0, kt)
    for g in range(n_g):
        p_prev = None
        for kt in range(n_kt):
            if g + 1 < n_g:
                score_tile(g + 1, kt)
            p = jnp.exp2(s_ref[g % 2, kt * key_tile:(kt + 1) * key_tile, :] - maxes[g]).astype(BF16)
            if kt > 0:
                value_tile(g, kt - 1, p_prev)
            p_prev = p
        value_tile(g, n_kt - 1, p_prev)
    return [a[0:HEAD_DIM] * (1.0 / a[HEAD_DIM:HEAD_DIM + 1]) for a in accs]


def _store_values_t(vt_ref, v):
    v_t = v.T
    for h in range(vt_ref.shape[0]):
        vt_ref[h, 0:HEAD_DIM, :] = v_t[h * HEAD_DIM:(h + 1) * HEAD_DIM, :].astype(BF16)
        vt_ref[h, HEAD_DIM:V_ROWS, :] = jnp.ones((V_ROWS - HEAD_DIM, v.shape[0]), BF16)


def _attn_kernel(aq_ref, ak_ref, av_ref, cq_ref, ck_ref, cv_ref,
                 a_cos_q, a_sa_q, a_sb_q, a_cos_k, a_sa_k, a_sb_k,
                 c_cos_q, c_sa_q, c_sb_q, c_cos_k, c_sa_k, c_sb_k,
                 lp_ref, gain_ref, qg_ref, kg_ref, bd_ref, oa_ref, oc_ref,
                 akr_ref, avt_ref, ckr_ref, cvt_ref, s_ref, *, tc, lam_init):
    t = pl.program_id(1)
    n_tok = akr_ref.shape[0]
    half = GROUP // 2
    bd = bd_ref[...]

    @pl.when(t == 0)
    def _():
        kr = _rope(ak_ref[0], a_cos_k[...], a_sa_k[...], a_sb_k[...], A_QK // 4)
        akr_ref[...] = kr.astype(BF16)
        _store_values_t(avt_ref, av_ref[0])
        k = ck_ref[0]
        kn = k * lax.rsqrt(_group_mean_sq(k, bd[0:half, 0:half]) + RMS_EPS) * kg_ref[...]
        ckr_ref[...] = _rope(kn, c_cos_k[...], c_sa_k[...], c_sb_k[...], HEAD_DIM // 4).astype(BF16)
        _store_values_t(cvt_ref, cv_ref[0])

    lp = lp_ref[...]
    lam = (jnp.exp(jnp.sum(lp[0:1] * lp[1:2], axis=-1, keepdims=True))
           - jnp.exp(jnp.sum(lp[2:3] * lp[3:4], axis=-1, keepdims=True)) + lam_init)
    lane = lax.broadcasted_iota(jnp.int32, (1, GROUP), 1)
    lane_h = lax.broadcasted_iota(jnp.int32, (1, half), 1)
    n_heads = GROUP // HEAD_DIM

    def attend(lo, nk, n_sub):
        groups = []
        for u in range(n_sub):
            r = slice(u * TILE, (u + 1) * TILE)
            qa = _rope(aq_ref[0, r, :], a_cos_q[r, :], a_sa_q[r, :], a_sb_q[r, :], A_QK // 4)
            qa = qa * (A_QK ** -0.5 * LOG2E)
            q = cq_ref[0, r, :]
            qn = q * lax.rsqrt(_group_mean_sq(q, bd) + RMS_EPS) * qg_ref[...]
            qc = _rope(qn, c_cos_q[r, :], c_sa_q[r, :], c_sb_q[r, :], HEAD_DIM // 4)
            qc = qc * (HEAD_DIM ** -0.5 * LOG2E)
            groups += [(akr_ref, jnp.concatenate([jnp.where(lane // A_QK == 2 * h + c, qa, 0.0).astype(BF16)
                                                  for c in range(2)], axis=0), avt_ref, h)
                       for h in range(n_heads)]
            groups += [(ckr_ref, jnp.concatenate(
                [jnp.where(lane_h // HEAD_DIM == g, qc[:, qb * half:(qb + 1) * half], 0.0).astype(BF16)
                 for qb in range(2)], axis=0), cvt_ref, g) for g in range(2)]
        outs = _attend_t(groups, s_ref, lo, nk)
        per_sub = n_heads + 2
        for u in range(n_sub):
            r = slice(u * TILE, (u + 1) * TILE)
            o = outs[u * per_sub:(u + 1) * per_sub]
            heads = [x[:, 0:TILE] - lam * x[:, TILE:2 * TILE] for x in o[0:n_heads]]
            acc = jnp.concatenate(heads, axis=0).T
            y = acc * lax.rsqrt(_group_mean_sq(acc, bd) + RMS_EPS)
            oa_ref[0, r, :] = y * gain_ref[...] * (1.0 - lam_init)
            heads = [o[n_heads + g][:, qb * TILE:(qb + 1) * TILE] for qb in range(2) for g in range(2)]
            oc_ref[0, r, :] = jnp.concatenate(heads, axis=0).T

    @pl.when(t == pl.num_programs(1) - 1)
    def _():
        attend(n_tok - tc, tc, 1)

    @pl.when(t < pl.num_programs(1) - 1)
    def _():
        attend(0, n_tok, aq_ref.shape[1] // TILE)


def _attention(proj, rope_a, rope_c, lam_params, sub_gain, q_gain, k_gain, bd, tc, lam_init):
    b, t, _ = proj.shape
    half = GROUP // 2
    gain = jnp.tile(sub_gain, GROUP // HEAD_DIM).reshape(1, GROUP)
    qg = jnp.tile(q_gain, GROUP // HEAD_DIM).reshape(1, GROUP)
    kg = jnp.tile(k_gain, half // HEAD_DIM).reshape(1, half)

    tq = 2 * TILE
    assert (t - tc) % tq == 0 and tc == TILE

    def rows(width, col):
        return pl.BlockSpec((1, tq, width), lambda i, j: (i, j, col // width))

    def sample(width, col):
        return pl.BlockSpec((1, t, width), lambda i, j: (i, 0, col // width))

    tile_spec = pl.BlockSpec((tq, GROUP), lambda i, j: (j, 0))
    out_spec = pl.BlockSpec((1, tq, GROUP), lambda i, j: (i, j, 0))
    return pl.pallas_call(
        functools.partial(_attn_kernel, tc=tc, lam_init=lam_init),
        grid=(b, (t - tc) // tq + 1),
        in_specs=[rows(GROUP, COL_A_Q), sample(GROUP, COL_A_K), sample(GROUP, COL_A_V),
                  rows(GROUP, COL_C_Q), sample(half, COL_C_K), sample(half, COL_C_V),
                  tile_spec, tile_spec, tile_spec,
                  _resident((t, GROUP)), _resident((t, GROUP)), _resident((t, GROUP)),
                  tile_spec, tile_spec, tile_spec,
                  _resident((t, half)), _resident((t, half)), _resident((t, half)),
                  _resident(lam_params.shape), _resident((1, GROUP)), _resident((1, GROUP)),
                  _resident((1, half)), _resident((GROUP, GROUP))],
        out_specs=[out_spec, out_spec],
        out_shape=[jax.ShapeDtypeStruct((b, t, GROUP), F32)] * 2,
        scratch_shapes=[pltpu.VMEM((t, GROUP), BF16),
                        pltpu.VMEM((GROUP // HEAD_DIM, V_ROWS, t), BF16),
                        pltpu.VMEM((t, half), BF16),
                        pltpu.VMEM((half // HEAD_DIM, V_ROWS, t), BF16),
                        pltpu.VMEM((2, t, 2 * TILE), F32)],
        compiler_params=_cparams(2),
        name="attention",
    )(proj, proj, proj, proj, proj, proj, *rope_a, *rope_a, *rope_c, *rope_c,
      lam_params, gain, qg, kg, bd)


def _fourier_stage_kernel(ux_ref, uc_ref, c64_ref, dctx_ref, w_ref, o_ref):
    s = ux_ref.shape[1]
    tc = uc_ref.shape[1]
    w = _dot(ux_ref[0].astype(BF16), c64_ref[...])
    w_ref[0:s, :] = w[:, 0:GROUP].astype(BF16)
    w_ref[s:2 * s, :] = w[:, GROUP:2 * GROUP].astype(BF16)
    wc = _dot(uc_ref[0].astype(BF16), c64_ref[...])
    wc = jnp.concatenate([wc[:, 0:GROUP], wc[:, GROUP:2 * GROUP]], axis=0).astype(BF16)
    o_ref[0] = _dot(dctx_ref[...], wc) * (tc ** -0.5)


def _fourier_dft_kernel(d_ref, w_ref, alias_ref, o_ref):
    del alias_ref
    o_ref[0] = _dot(d_ref[...], w_ref[...]) * ((d_ref.shape[1] // 2) ** -0.5)


def _fourier(proj, c64, dctx, dx, tc):
    b, t, _ = proj.shape
    s = t - tc
    col = COL_D_U // GROUP
    w_all, o_ctx = pl.pallas_call(
        _fourier_stage_kernel,
        grid=(b,),
        in_specs=[pl.BlockSpec((1, s, GROUP), lambda i: (i, 0, col)),
                  pl.BlockSpec((1, tc, GROUP), lambda i: (i, s // tc, col)),
                  pl.BlockSpec((GROUP, 2 * GROUP), lambda i: (0, 0)),
                  pl.BlockSpec((tc, 2 * tc), lambda i: (0, 0))],
        out_specs=[pl.BlockSpec((2 * s, GROUP), lambda i: (0, i)),
                   pl.BlockSpec((1, tc, GROUP), lambda i: (i, s // tc, 0))],
        out_shape=[jax.ShapeDtypeStruct((2 * s, b * GROUP), BF16),
                   jax.ShapeDtypeStruct((b, t, GROUP), F32)],
        compiler_params=_cparams(1),
        name="fourier_stage",
    )(proj, proj, c64, dctx)
    tm = next(c for c in (4 * TILE, 2 * TILE) if s % c == 0)
    return pl.pallas_call(
        _fourier_dft_kernel,
        grid=(s // tm, b),
        in_specs=[pl.BlockSpec((tm, 2 * s), lambda i, j: (i, 0)),
                  pl.BlockSpec((2 * s, GROUP), lambda i, j: (0, j)),
                  pl.BlockSpec(memory_space=pl.ANY)],
        out_specs=pl.BlockSpec((1, tm, GROUP), lambda i, j: (j, i, 0)),
        out_shape=jax.ShapeDtypeStruct((b, t, GROUP), F32),
        input_output_aliases={2: 0},
        compiler_params=_cparams(2),
        name="fourier_dft",
    )(dx, w_all, o_ctx)


def _dft_tables(tc, s):
    def cs_direct(rows, n, period, quarter_turn=False):
        j = jnp.arange(rows, dtype=jnp.int32)
        k = jnp.arange(n, dtype=jnp.int32)
        ang = ((j[:, None] * k[None, :]) % period).astype(F32) * (2.0 * math.pi / period)
        if quarter_turn:
            ang = jnp.concatenate([ang, ang + 0.5 * math.pi], axis=1)
        return jnp.cos(ang), jnp.sin(ang)

    def cos_neg_sin(n):
        ca, sa = cs_direct(n // GRID_W, n, n // GRID_W)
        ca, sa = jnp.tile(ca, (1, 2)), jnp.tile(sa, (1, 2))
        cb, sb = cs_direct(GRID_W, n, n, quarter_turn=True)
        cos = ca[:, None, :] * cb[None, :, :] - sa[:, None, :] * sb[None, :, :]
        return cos.reshape(n, 2 * n).astype(BF16)

    c64, s64 = cs_direct(HEAD_DIM, HEAD_DIM, HEAD_DIM)
    eye = jnp.eye(GROUP // HEAD_DIM, dtype=F32)
    c64 = jnp.concatenate([jnp.kron(eye, c64), jnp.kron(eye, s64)], axis=1) * (HEAD_DIM ** -0.5)
    return c64.astype(BF16), cos_neg_sin(tc), cos_neg_sin(s)


def _hgrn_tables():
    c = CHUNK
    idx = np.arange(c)
    tri = (idx[None, :] <= idx[:, None]).astype(np.float32)
    lvl = np.full((c, c), -1.0, np.float32)
    lvl[idx, idx] = 0.0
    rows = [tri]
    for l in range(1, LEVELS + 1):
        w = c >> (l - 1)
        blk, pos = idx // w, idx % w
        ref = blk * w + w // 2 - 1
        rows.append(tri[ref])
        m = (blk[:, None] == blk[None, :]) & (pos[:, None] >= w // 2) & (pos[None, :] < w // 2)
        lvl[m] = float(l)
    w_f = np.concatenate(rows, axis=0)
    w_b = w_f.reshape(LEVELS + 1, c, c)[:, ::-1, ::-1].reshape(-1, c)
    heads = GROUP // HEAD_DIM
    head_mask = np.kron(np.eye(heads, dtype=np.float32), np.ones((c, HEAD_DIM), np.float32))
    return (np.tile(w_f, (1, 3)), np.tile(w_b, (1, 3)),
            np.tile(lvl, (1, heads)), np.tile(lvl.T, (1, heads)), head_mask)


HGRN_FAST_RANGE = 75.0


def _hgrn_gates(z, lb):
    e = jnp.exp(-jnp.abs(z))
    r = 1.0 / (1.0 + e)
    sig_pos = jnp.where(z >= 0, r, e * r)
    sig_neg = jnp.where(z >= 0, e * r, r)
    return jnp.log(lb + (1.0 - lb) * sig_pos), (1.0 - lb) * sig_neg


def _split3(g):
    g_hi = g.astype(BF16)
    r1 = g - g_hi.astype(F32)
    g_mid = r1.astype(BF16)
    g_lo = (r1 - g_mid.astype(F32)).astype(BF16)
    return jnp.concatenate([g_hi, g_mid, g_lo], axis=0)


def _block_diag(x, hm_b):
    return jnp.concatenate([x.astype(BF16)] * (GROUP // HEAD_DIM), axis=0) * hm_b


def _hgrn_output(ch, a, hm, hm_b, st_ref, end_row):
    q, k, v, b = ch["q"], ch["k"], ch["v"], ch["b"]
    b_end = b[end_row:end_row + 1]
    st = st_ref[...]
    o = _dot(a.astype(BF16), _block_diag(v, hm_b))
    o = o + _dot_nt((q * jnp.exp(b)).astype(BF16), st.astype(BF16))
    upd = _dot_tn(v.astype(BF16), (k * jnp.exp(b_end - b)).astype(BF16))
    st_ref[...] = st * jnp.exp(b_end) + upd * hm
    return o


def _hgrn_scores_fast(ch, lvl, hm_b):
    d = ch["b"] - ch["b_mid"]
    a = _dot_nt((ch["q"] * jnp.exp(d)).astype(BF16), _block_diag(ch["k"] * jnp.exp(-d), hm_b))
    return jnp.where(lvl >= 0.0, a, 0.0)


def _hgrn_scores_levels(ch, w3, lvl, hm_b):
    c = CHUNK
    q, k = ch["q"], ch["k"]
    sums = _dot(w3, ch["g3"])
    b = sums[0:c]
    a = jnp.where(lvl == 0.0, _dot_nt(q.astype(BF16), _block_diag(k, hm_b)), 0.0)
    for l in range(1, LEVELS + 1):
        d = jnp.exp(-jnp.abs(b - sums[l * c:(l + 1) * c]))
        a = jnp.where(lvl == float(l), _dot_nt((q * d).astype(BF16), _block_diag(k * d, hm_b)), a)
    return a


def _hgrn_kernel(qf_ref, zf_ref, vf_ref, qb_ref, zb_ref, vb_ref, lbf_ref, lbb_ref,
                 w3f_ref, w3b_ref, lvf_ref, lvb_ref, hm_ref, of_ref, ob_ref, sf_ref, sb_ref):
    @pl.when(pl.program_id(1) == 0)
    def _():
        sf_ref[...] = jnp.zeros_like(sf_ref)
        sb_ref[...] = jnp.zeros_like(sb_ref)

    hm = hm_ref[...]
    hm_b = hm.astype(BF16)
    n = TILE // CHUNK
    fwd = dict(q=qf_ref, z=zf_ref, v=vf_ref, lb=lbf_ref, w3=w3f_ref, lvl=lvf_ref, o=of_ref, st=sf_ref,
               end=CHUNK - 1)
    bwd = dict(q=qb_ref, z=zb_ref, v=vb_ref, lb=lbb_ref, w3=w3b_ref, lvl=lvb_ref, o=ob_ref, st=sb_ref,
               end=0)
    order = [(d, ci if d is fwd else n - 1 - ci) for ci in range(n) for d in (fwd, bwd)]

    chunks = []
    spread = None
    for d, ci in order:
        rows = slice(ci * CHUNK, (ci + 1) * CHUNK)
        g, k = _hgrn_gates(d["z"][0, rows, :], d["lb"][...])
        g3 = _split3(g)
        sums = _dot(d["w3"][0:2 * CHUNK, :], g3)
        ch = dict(q=d["q"][0, rows, :], v=d["v"][0, rows, :], k=k, g3=g3,
                  b=sums[0:CHUNK], b_mid=sums[CHUNK:2 * CHUNK])
        dist = jnp.abs(ch["b"] - ch["b_mid"])
        spread = dist if spread is None else jnp.maximum(spread, dist)
        chunks.append(ch)
    fast = jnp.max(spread) < HGRN_FAST_RANGE

    @pl.when(fast)
    def _():
        for (d, ci), ch in zip(order, chunks):
            a = _hgrn_scores_fast(ch, d["lvl"][...], hm_b)
            d["o"][0, ci * CHUNK:(ci + 1) * CHUNK, :] = _hgrn_output(ch, a, hm, hm_b, d["st"], d["end"])

    @pl.when(jnp.logical_not(fast))
    def _():
        for (d, ci), ch in zip(order, chunks):
            a = _hgrn_scores_levels(ch, d["w3"][...], d["lvl"][...], hm_b)
            d["o"][0, ci * CHUNK:(ci + 1) * CHUNK, :] = _hgrn_output(ch, a, hm, hm_b, d["st"], d["end"])


def _hgrn(proj, lb_f, lb_b, tables):
    b, t, _ = proj.shape
    nt = t // TILE
    w3f, w3b, lvf, lvb, hm = tables

    def fwd(col):
        return pl.BlockSpec((1, TILE, GROUP),
                            lambda i, j: (i, jnp.where(j == 0, nt - 1, j - 1), col // GROUP))

    def bwd(col):
        return pl.BlockSpec((1, TILE, GROUP), lambda i, j: (i, nt - 1 - j, col // GROUP))

    def const(a):
        return pl.BlockSpec(a.shape, lambda i, j: (0,) * a.ndim)

    out_f = fwd(0)
    out_b = bwd(0)
    lb_f = lb_f.reshape(1, GROUP)
    lb_b = lb_b.reshape(1, GROUP)
    return pl.pallas_call(
        _hgrn_kernel,
        grid=(b, nt),
        in_specs=[fwd(COL_B_Q), fwd(COL_B_FF), fwd(COL_B_I), bwd(COL_B_Q), bwd(COL_B_FB), bwd(COL_B_I),
                  const(lb_f), const(lb_b), const(w3f), const(w3b), const(lvf), const(lvb), const(hm)],
        out_specs=[out_f, out_b],
        out_shape=[jax.ShapeDtypeStruct((b, t, GROUP), F32)] * 2,
        scratch_shapes=[pltpu.VMEM((GROUP, GROUP), F32), pltpu.VMEM((GROUP, GROUP), F32)],
        compiler_params=_cparams(2),
        name="hgrn2",
    )(proj, proj, proj, proj, proj, proj, lb_f, lb_b, w3f, w3b, lvf, lvb, hm)


def _outffn_kernel(x_ref, ma_ref, of_ref, ob_ref, g_ref, mc_ref, md_ref, mod_ref, hg_ref, n2_ref,
                   fn_ref, bd_ref, wo_ref, wi_ref, wd_ref, o_ref, act_ref, *, hidden, final):
    n_sub = x_ref.shape[1] // TILE
    o = of_ref[0] + ob_ref[0]
    g = g_ref[0]
    mb = (o * lax.rsqrt(_group_mean_sq(o, bd_ref[...]) + RMS_EPS) * hg_ref[...]) * (g * jax.nn.sigmoid(g))
    mix = jnp.concatenate([ma_ref[0], mb, mc_ref[0], md_ref[0]], axis=-1).astype(BF16)
    y = _dot(mix, wo_ref[...])
    xs, hs, gates = [], [], []
    for i in range(n_sub):
        m = _mod_rows(mod_ref, i, n_sub, not final)
        x = x_ref[0, i * TILE:(i + 1) * TILE, :] + m[2:3] * y[i * TILE:(i + 1) * TILE]
        ms = jnp.mean(x * x, axis=-1, keepdims=True)
        xs.append(x)
        hs.append((x * lax.rsqrt(ms + RMS_EPS) * (n2_ref[...] * (1.0 + m[4:5])) + m[3:4]).astype(BF16))
        gates.append(m[5:6])
    h = jnp.concatenate(hs, axis=0)
    step = 256
    for c in range(hidden // step):
        gate = _dot(h, wi_ref[:, c * step:(c + 1) * step])
        up = _dot(h, wi_ref[:, hidden + c * step:hidden + (c + 1) * step])
        act_ref[:, c * step:(c + 1) * step] = (gate * jax.nn.sigmoid(gate) * up).astype(BF16)
    acc = _dot(act_ref[...], wd_ref[...])
    for i in range(n_sub):
        x = xs[i] + gates[i] * acc[i * TILE:(i + 1) * TILE]
        if final:
            ms = jnp.mean(x * x, axis=-1, keepdims=True)
            x = x * lax.rsqrt(ms + RMS_EPS) * fn_ref[...]
        o_ref[0, i * TILE:(i + 1) * TILE, :] = x


def _out_ffn(xs, ma, o_f, o_b, proj, mc, md, modsel, hgrn_gain, norm2, final_norm, bd, w_out,
             w_ffn_in, w_ffn_out, n_rows, final):
    b, _, d = xs.shape
    hidden = w_ffn_out.shape[0]
    tm = _row_tile(n_rows) if not final else 2 * TILE

    def rows(width, col=0):
        return pl.BlockSpec((1, tm, width), lambda i, j: (i, j, col))

    hg = jnp.tile(hgrn_gain, GROUP // HEAD_DIM).reshape(1, GROUP)
    return pl.pallas_call(
        functools.partial(_outffn_kernel, hidden=hidden, final=final),
        grid=(b, n_rows // tm),
        in_specs=[rows(d), rows(GROUP), rows(GROUP), rows(GROUP), rows(GROUP, COL_B_G // GROUP),
                  rows(GROUP), rows(GROUP),
                  pl.BlockSpec((1, 2, 6, d), lambda i, j: (i, 0, 0, 0)),
                  _resident((1, GROUP)), _resident((1, d)), _resident((1, d)),
                  _resident((GROUP, GROUP)), _resident(w_out.shape), _resident(w_ffn_in.shape),
                  _resident(w_ffn_out.shape)],
        out_specs=rows(d),
        out_shape=jax.ShapeDtypeStruct((b, n_rows, d), F32),
        scratch_shapes=[pltpu.VMEM((tm, hidden), BF16)],
        compiler_params=_cparams(2),
        name="out_ffn",
    )(xs, ma, o_f, o_b, proj, mc, md, modsel, hg, norm2.reshape(1, d), final_norm.reshape(1, d), bd,
      w_out, w_ffn_in, w_ffn_out)


def _rope_tables(tc, s, dim, width):
    pos = jnp.arange(s)
    row = (pos // GRID_W).astype(F32)
    col = (pos % GRID_W).astype(F32)
    n_freq = dim // 4
    inv_freq = ROPE_THETA ** (-jnp.arange(n_freq, dtype=F32) / n_freq)
    lane = np.arange(width) % dim
    is_col = (lane // (dim // 2)) == 1
    second = ((lane % (dim // 2)) // n_freq) == 1
    freq = inv_freq[lane % n_freq]
    ang = jnp.where(is_col[None, :], col[:, None], row[:, None]) * freq[None, :]
    cos, sin = jnp.cos(ang), jnp.sin(ang)
    sin_a = jnp.where(second[None, :], sin, 0.0)
    sin_b = jnp.where(second[None, :], 0.0, -sin)
    ident = jnp.ones((tc, width), F32)
    zero = jnp.zeros((tc, width), F32)
    return (jnp.concatenate([cos, ident], axis=0), jnp.concatenate([sin_a, zero], axis=0),
            jnp.concatenate([sin_b, zero], axis=0))


def kernel(x, c, ctx, c_ctx, w_mod, b_mod, norm1, w_in, diff_lambda, diff_norm, hgrn_lb_logits,
           hgrn_norm, q_norm, k_norm, w_out, norm2, w_ffn_in, w_ffn_out, final_norm):
    b, s, d = x.shape
    tc = ctx.shape[1]
    depth = w_mod.shape[0]
    assert tc == TILE and s % TILE == 0 and s % GRID_W == 0 and d == 4 * GROUP

    rope_a = _rope_tables(tc, s, A_QK, GROUP)
    rope_c = _rope_tables(tc, s, HEAD_DIM, GROUP)
    dft = _dft_tables(tc, s)
    hgrn_tabs = _hgrn_tables()
    hgrn_tabs = tuple(jnp.asarray(a, BF16 if i < 2 else F32) for i, a in enumerate(hgrn_tabs))
    bd = jnp.asarray(np.kron(np.eye(GROUP // HEAD_DIM), np.full((HEAD_DIM, HEAD_DIM), 1.0 / HEAD_DIM)), BF16)

    p = jax.nn.softmax(hgrn_lb_logits.astype(F32), axis=1)
    lower = jnp.cumsum(p, axis=1) - p[:, :1]

    def swap_heads(w, axis, start):
        cuts = [0, start + HEAD_DIM, start + 2 * HEAD_DIM, start + 3 * HEAD_DIM, w.shape[axis]]
        parts = [lax.slice_in_dim(w, cuts[i], cuts[i + 1], axis=axis) for i in range(4)]
        return jnp.concatenate([parts[0], parts[2], parts[1], parts[3]], axis=axis).astype(BF16)

    w_in_b = swap_heads(w_in, 2, COL_C_Q)
    w_out_b = swap_heads(w_out, 1, 2 * GROUP)
    w_ffn_in_b = w_ffn_in.astype(BF16)
    w_ffn_out_b = w_ffn_out.astype(BF16)

    rows = -(-(b + 1) // 8) * 8
    cc = jnp.zeros((rows, d), F32).at[:b].set(c).at[b].set(c_ctx)
    mod = _modulation(cc, w_mod, b_mod).reshape(depth, rows, 6, d)
    modsel = jnp.stack([jnp.broadcast_to(mod[:, b:b + 1], (depth, b, 6, d)), mod[:, :b]], axis=2)

    xs = jnp.concatenate([x, ctx], axis=1)
    for l in range(depth):
        last = l == depth - 1
        lam_init = 0.8 - 0.6 * math.exp(-0.3 * l)
        proj = _in_proj(xs, modsel[l], norm1[l], w_in_b[l])
        m_a, m_c = _attention(proj, rope_a, rope_c, diff_lambda[l], diff_norm[l], q_norm[l], k_norm[l],
                              bd, tc, lam_init)
        o_f, o_b = _hgrn(proj, lower[0, l], lower[1, l], hgrn_tabs)
        m_d = _fourier(proj, *dft, tc)
        xs = _out_ffn(xs, m_a, o_f, o_b, proj, m_c, m_d, modsel[l], hgrn_norm[l], norm2[l], final_norm,
                      bd, w_out_b[l], w_ffn_in_b[l], w_ffn_out_b[l], s if last else s + tc, last)
    return xs
```

```python
import functools
import math

import numpy as np
import jax
import jax.numpy as jnp
from jax import lax
from jax.experimental import pallas as pl
from jax.experimental.pallas import tpu as pltpu

F32 = jnp.float32
BF16 = jnp.bfloat16

HEAD_DIM = 64
GRID_W = 64
ROPE_THETA = 10000.0
RMS_EPS = 1e-6
GROUP = 256
A_QK = 32
TILE = 256
CHUNK = 64
LEVELS = 6
LOG2E = 1.4426950408889634
VMEM_LIMIT = 56 * 1024 * 1024

COL_A_Q, COL_A_K, COL_A_V = 0, 256, 512
COL_B_Q, COL_B_FF, COL_B_FB, COL_B_I, COL_B_G = 768, 1024, 1280, 1536, 1792
COL_C_Q, COL_C_K, COL_C_V = 2048, 2304, 2432
COL_D_U = 2560


def _cparams(n_axes):
    return pltpu.CompilerParams(dimension_semantics=("arbitrary",) * n_axes,
                                vmem_limit_bytes=VMEM_LIMIT)


def _dot(a, b):
    return jnp.dot(a, b, preferred_element_type=F32)


def _dot_nt(a, b):
    return lax.dot_general(a, b, (((1,), (1,)), ((), ())), preferred_element_type=F32)


def _dot_tn(a, b):
    return lax.dot_general(a, b, (((0,), (0,)), ((), ())), preferred_element_type=F32)


def _group_mean_sq(y, bd):
    y2 = y * y
    hi = y2.astype(BF16)
    lo = (y2 - hi.astype(F32)).astype(BF16)
    return _dot(hi, bd) + _dot(lo, bd)


def _rope(x, cos, sin_a, sin_b, shift):
    w = x.shape[-1]
    return x * cos + pltpu.roll(x, shift, 1) * sin_a + pltpu.roll(x, w - shift, 1) * sin_b


def _mod_kernel(c_ref, w_ref, b_ref, o_ref):
    c = c_ref[...]
    a = c * jax.nn.sigmoid(c)
    o_ref[0] = _dot(a.astype(BF16), w_ref[0].astype(BF16)) + b_ref[0]


def _modulation(cc, w_mod, b_mod):
    depth, d, n = w_mod.shape
    r = cc.shape[0]
    tn = 1536
    return pl.pallas_call(
        _mod_kernel,
        grid=(depth, n // tn),
        in_specs=[pl.BlockSpec((r, d), lambda l, j: (0, 0)),
                  pl.BlockSpec((1, d, tn), lambda l, j: (l, 0, j)),
                  pl.BlockSpec((1, 1, tn), lambda l, j: (l, 0, j))],
        out_specs=pl.BlockSpec((1, r, tn), lambda l, j: (l, 0, j)),
        out_shape=jax.ShapeDtypeStruct((depth, r, n), F32),
        compiler_params=_cparams(2),
        name="modulation",
    )(cc, w_mod, b_mod.reshape(depth, 1, n))


def _mod_rows(mod_ref, sub, n_sub, with_ctx):
    m = mod_ref[0]
    if not (with_ctx and sub == n_sub - 1):
        return m[1]
    return jnp.where(pl.program_id(1) == pl.num_programs(1) - 1, m[0], m[1])


def _inproj_kernel(x_ref, mod_ref, n1_ref, w_ref, o_ref):
    n_sub = x_ref.shape[1] // TILE
    hs = []
    for i in range(n_sub):
        x = x_ref[0, i * TILE:(i + 1) * TILE, :]
        ms = jnp.mean(x * x, axis=-1, keepdims=True)
        y = x * lax.rsqrt(ms + RMS_EPS) * n1_ref[...]
        m = _mod_rows(mod_ref, i, n_sub, True)
        hs.append((y * (1.0 + m[1:2]) + m[0:1]).astype(BF16))
    o_ref[0] = _dot(jnp.concatenate(hs, axis=0), w_ref[...])


def _row_tile(t):
    return next(tm for tm in (3 * TILE, 2 * TILE, TILE) if t % tm == 0)


def _resident(shape):
    return pl.BlockSpec(shape, lambda i, j: (0,) * len(shape), pipeline_mode=pl.Buffered(1))


def _layer_resident(w, layer):
    return pl.BlockSpec((None,) + w.shape[1:], lambda i, j: (layer, 0, 0), pipeline_mode=pl.Buffered(1))


def _in_proj(xs, modsel, norm1, w_in, layer):
    b, t, d = xs.shape
    n = w_in.shape[2]
    tm = _row_tile(t)
    return pl.pallas_call(
        _inproj_kernel,
        grid=(b, t // tm),
        in_specs=[pl.BlockSpec((1, tm, d), lambda i, j: (i, j, 0)),
                  pl.BlockSpec((1, 2, 6, d), lambda i, j: (i, 0, 0, 0)),
                  _resident((1, d)), _layer_resident(w_in, layer)],
        out_specs=pl.BlockSpec((1, tm, n), lambda i, j: (i, j, 0)),
        out_shape=jax.ShapeDtypeStruct((b, t, n), F32),
        compiler_params=_cparams(2),
        name="in_proj",
    )(xs, modsel, norm1.reshape(1, d), w_in)


V_ROWS = HEAD_DIM + 16


KEY_TILES = (768, 256)


def _attend_t(groups, s_ref, lo, nk):
    key_tile = next(c for c in KEY_TILES if nk % c == 0)
    n_kt = nk // key_tile
    n_g = len(groups)
    maxes = [None] * n_g
    accs = [None] * n_g

    def keys(kt):
        return slice(lo + kt * key_tile, lo + (kt + 1) * key_tile)

    def score_tile(g, kt):
        kr_ref, wq, _, _ = groups[g]
        s = _dot_nt(kr_ref[keys(kt), :], wq)
        s_ref[g % 2, kt * key_tile:(kt + 1) * key_tile, :] = s
        m = jnp.max(s, axis=0, keepdims=True)
        maxes[g] = m if kt == 0 else jnp.maximum(maxes[g], m)

    def value_tile(g, kt, p):
        _, _, vt_ref, head = groups[g]
        pv = _dot(vt_ref[head, :, keys(kt)], p)
        accs[g] = pv if kt == 0 else accs[g] + pv

    for kt in range(n_kt):
        score_tile(0, kt)
    for g in range(n_g):
        p_prev = None
        for kt in range(n_kt):
            if g + 1 < n_g:
                score_tile(g + 1, kt)
            p = jnp.exp2(s_ref[g % 2, kt * key_tile:(kt + 1) * key_tile, :] - maxes[g]).astype(BF16)
            if kt > 0:
                value_tile(g, kt - 1, p_prev)
            p_prev = p
        value_tile(g, n_kt - 1, p_prev)
    return [a[0:HEAD_DIM] * (1.0 / a[HEAD_DIM:HEAD_DIM + 1]) for a in accs]


def _store_values_t(vt_ref, v):
    v_t = v.T
    for h in range(vt_ref.shape[0]):
        vt_ref[h, 0:HEAD_DIM, :] = v_t[h * HEAD_DIM:(h + 1) * HEAD_DIM, :].astype(BF16)
        vt_ref[h, HEAD_DIM:V_ROWS, :] = jnp.ones((V_ROWS - HEAD_DIM, v.shape[0]), BF16)


def _attn_kernel(aq_ref, ak_ref, av_ref, cq_ref, ck_ref, cv_ref,
                 a_cos_q, a_sa_q, a_sb_q, a_cos_k, a_sa_k, a_sb_k,
                 c_cos_q, c_sa_q, c_sb_q, c_cos_k, c_sa_k, c_sb_k,
                 lp_ref, gain_ref, qg_ref, kg_ref, bd_ref, oa_ref, oc_ref,
                 akr_ref, avt_ref, ckr_ref, cvt_ref, s_ref, *, tc, lam_init):
    t = pl.program_id(1)
    n_tok = akr_ref.shape[0]
    half = GROUP // 2
    bd = bd_ref[...]

    @pl.when(t == 0)
    def _():
        kr = _rope(ak_ref[0], a_cos_k[...], a_sa_k[...], a_sb_k[...], A_QK // 4)
        akr_ref[...] = kr.astype(BF16)
        _store_values_t(avt_ref, av_ref[0])
        k = ck_ref[0]
        kn = k * lax.rsqrt(_group_mean_sq(k, bd[0:half, 0:half]) + RMS_EPS) * kg_ref[...]
        ckr_ref[...] = _rope(kn, c_cos_k[...], c_sa_k[...], c_sb_k[...], HEAD_DIM // 4).astype(BF16)
        _store_values_t(cvt_ref, cv_ref[0])

    lp = lp_ref[...]
    lam = (jnp.exp(jnp.sum(lp[0:1] * lp[1:2], axis=-1, keepdims=True))
           - jnp.exp(jnp.sum(lp[2:3] * lp[3:4], axis=-1, keepdims=True)) + lam_init)
    lane = lax.broadcasted_iota(jnp.int32, (1, GROUP), 1)
    lane_h = lax.broadcasted_iota(jnp.int32, (1, half), 1)
    n_heads = GROUP // HEAD_DIM

    def attend(lo, nk, n_sub):
        groups = []
        for u in range(n_sub):
            r = slice(u * TILE, (u + 1) * TILE)
            qa = _rope(aq_ref[0, r, :], a_cos_q[r, :], a_sa_q[r, :], a_sb_q[r, :], A_QK // 4)
            qa = qa * (A_QK ** -0.5 * LOG2E)
            q = cq_ref[0, r, :]
            qn = q * lax.rsqrt(_group_mean_sq(q, bd) + RMS_EPS) * qg_ref[...]
            qc = _rope(qn, c_cos_q[r, :], c_sa_q[r, :], c_sb_q[r, :], HEAD_DIM // 4)
            qc = qc * (HEAD_DIM ** -0.5 * LOG2E)
            groups += [(akr_ref, jnp.concatenate([jnp.where(lane // A_QK == 2 * h + c, qa, 0.0).astype(BF16)
                                                  for c in range(2)], axis=0), avt_ref, h)
                       for h in range(n_heads)]
            groups += [(ckr_ref, jnp.concatenate(
                [jnp.where(lane_h // HEAD_DIM == g, qc[:, qb * half:(qb + 1) * half], 0.0).astype(BF16)
                 for qb in range(2)], axis=0), cvt_ref, g) for g in range(2)]
        outs = _attend_t(groups, s_ref, lo, nk)
        per_sub = n_heads + 2
        for u in range(n_sub):
            r = slice(u * TILE, (u + 1) * TILE)
            o = outs[u * per_sub:(u + 1) * per_sub]
            heads = [x[:, 0:TILE] - lam * x[:, TILE:2 * TILE] for x in o[0:n_heads]]
            acc = jnp.concatenate(heads, axis=0).T
            y = acc * lax.rsqrt(_group_mean_sq(acc, bd) + RMS_EPS)
            oa_ref[0, r, :] = y * gain_ref[...] * (1.0 - lam_init)
            heads = [o[n_heads + g][:, qb * TILE:(qb + 1) * TILE] for qb in range(2) for g in range(2)]
            oc_ref[0, r, :] = jnp.concatenate(heads, axis=0).T

    @pl.when(t == pl.num_programs(1) - 1)
    def _():
        attend(n_tok - tc, tc, 1)

    @pl.when(t < pl.num_programs(1) - 1)
    def _():
        attend(0, n_tok, aq_ref.shape[1] // TILE)


def _attention(proj, rope_a, rope_c, lam_params, sub_gain, q_gain, k_gain, bd, tc, lam_init):
    b, t, _ = proj.shape
    half = GROUP // 2
    gain = jnp.tile(sub_gain, GROUP // HEAD_DIM).reshape(1, GROUP)
    qg = jnp.tile(q_gain, GROUP // HEAD_DIM).reshape(1, GROUP)
    kg = jnp.tile(k_gain, half // HEAD_DIM).reshape(1, half)

    tq = 2 * TILE
    assert (t - tc) % tq == 0 and tc == TILE

    def rows(width, col):
        return pl.BlockSpec((1, tq, width), lambda i, j: (i, j, col // width))

    def sample(width, col):
        return pl.BlockSpec((1, t, width), lambda i, j: (i, 0, col // width))

    tile_spec = pl.BlockSpec((tq, GROUP), lambda i, j: (j, 0))
    out_spec = pl.BlockSpec((1, tq, GROUP), lambda i, j: (i, j, 0))
    return pl.pallas_call(
        functools.partial(_attn_kernel, tc=tc, lam_init=lam_init),
        grid=(b, (t - tc) // tq + 1),
        in_specs=[rows(GROUP, COL_A_Q), sample(GROUP, COL_A_K), sample(GROUP, COL_A_V),
                  rows(GROUP, COL_C_Q), sample(half, COL_C_K), sample(half, COL_C_V),
                  tile_spec, tile_spec, tile_spec,
                  _resident((t, GROUP)), _resident((t, GROUP)), _resident((t, GROUP)),
                  tile_spec, tile_spec, tile_spec,
                  _resident((t, half)), _resident((t, half)), _resident((t, half)),
                  _resident(lam_params.shape), _resident((1, GROUP)), _resident((1, GROUP)),
                  _resident((1, half)), _resident((GROUP, GROUP))],
        out_specs=[out_spec, out_spec],
        out_shape=[jax.ShapeDtypeStruct((b, t, GROUP), F32)] * 2,
        scratch_shapes=[pltpu.VMEM((t, GROUP), BF16),
                        pltpu.VMEM((GROUP // HEAD_DIM, V_ROWS, t), BF16),
                        pltpu.VMEM((t, half), BF16),
                        pltpu.VMEM((half // HEAD_DIM, V_ROWS, t), BF16),
                        pltpu.VMEM((2, t, 2 * TILE), F32)],
        compiler_params=_cparams(2),
        name="attention",
    )(proj, proj, proj, proj, proj, proj, *rope_a, *rope_a, *rope_c, *rope_c,
      lam_params, gain, qg, kg, bd)


def _fourier_stage_kernel(ux_ref, uc_ref, c64_ref, dctx_ref, w_ref, o_ref):
    s = ux_ref.shape[1]
    tc = uc_ref.shape[1]
    w = _dot(ux_ref[0].astype(BF16), c64_ref[...])
    w_ref[0:s, :] = w[:, 0:GROUP].astype(BF16)
    w_ref[s:2 * s, :] = w[:, GROUP:2 * GROUP].astype(BF16)
    wc = _dot(uc_ref[0].astype(BF16), c64_ref[...])
    wc = jnp.concatenate([wc[:, 0:GROUP], wc[:, GROUP:2 * GROUP]], axis=0).astype(BF16)
    o_ref[0] = _dot(dctx_ref[...], wc) * (tc ** -0.5)


def _fourier_dft_kernel(d_ref, w_ref, alias_ref, o_ref):
    del alias_ref
    o_ref[0] = _dot(d_ref[...], w_ref[...]) * ((d_ref.shape[1] // 2) ** -0.5)


def _fourier(proj, c64, dctx, dx, tc):
    b, t, _ = proj.shape
    s = t - tc
    col = COL_D_U // GROUP
    w_all, o_ctx = pl.pallas_call(
        _fourier_stage_kernel,
        grid=(b,),
        in_specs=[pl.BlockSpec((1, s, GROUP), lambda i: (i, 0, col)),
                  pl.BlockSpec((1, tc, GROUP), lambda i: (i, s // tc, col)),
                  pl.BlockSpec((GROUP, 2 * GROUP), lambda i: (0, 0)),
                  pl.BlockSpec((tc, 2 * tc), lambda i: (0, 0))],
        out_specs=[pl.BlockSpec((2 * s, GROUP), lambda i: (0, i)),
                   pl.BlockSpec((1, tc, GROUP), lambda i: (i, s // tc, 0))],
        out_shape=[jax.ShapeDtypeStruct((2 * s, b * GROUP), BF16),
                   jax.ShapeDtypeStruct((b, t, GROUP), F32)],
        compiler_params=_cparams(1),
        name="fourier_stage",
    )(proj, proj, c64, dctx)
    tm = next(c for c in (4 * TILE, 2 * TILE) if s % c == 0)
    return pl.pallas_call(
        _fourier_dft_kernel,
        grid=(s // tm, b),
        in_specs=[pl.BlockSpec((tm, 2 * s), lambda i, j: (i, 0)),
                  pl.BlockSpec((2 * s, GROUP), lambda i, j: (0, j)),
                  pl.BlockSpec(memory_space=pl.ANY)],
        out_specs=pl.BlockSpec((1, tm, GROUP), lambda i, j: (j, i, 0)),
        out_shape=jax.ShapeDtypeStruct((b, t, GROUP), F32),
        input_output_aliases={2: 0},
        compiler_params=_cparams(2),
        name="fourier_dft",
    )(dx, w_all, o_ctx)


def _dft_tables(tc, s):
    def cs_direct(rows, n, period, quarter_turn=False):
        j = jnp.arange(rows, dtype=jnp.int32)
        k = jnp.arange(n, dtype=jnp.int32)
        ang = ((j[:, None] * k[None, :]) % period).astype(F32) * (2.0 * math.pi / period)
        if quarter_turn:
            ang = jnp.concatenate([ang, ang + 0.5 * math.pi], axis=1)
        return jnp.cos(ang), jnp.sin(ang)

    def cos_neg_sin(n):
        ca, sa = cs_direct(n // GRID_W, n, n // GRID_W)
        ca, sa = jnp.tile(ca, (1, 2)), jnp.tile(sa, (1, 2))
        cb, sb = cs_direct(GRID_W, n, n, quarter_turn=True)
        cos = ca[:, None, :] * cb[None, :, :] - sa[:, None, :] * sb[None, :, :]
        return cos.reshape(n, 2 * n).astype(BF16)

    c64, s64 = cs_direct(HEAD_DIM, HEAD_DIM, HEAD_DIM)
    eye = jnp.eye(GROUP // HEAD_DIM, dtype=F32)
    c64 = jnp.concatenate([jnp.kron(eye, c64), jnp.kron(eye, s64)], axis=1) * (HEAD_DIM ** -0.5)
    return c64.astype(BF16), cos_neg_sin(tc), cos_neg_sin(s)


def _hgrn_tables():
    c = CHUNK
    idx = np.arange(c)
    tri = (idx[None, :] <= idx[:, None]).astype(np.float32)
    lvl = np.full((c, c), -1.0, np.float32)
    lvl[idx, idx] = 0.0
    rows = [tri]
    for l in range(1, LEVELS + 1):
        w = c >> (l - 1)
        blk, pos = idx // w, idx % w
        ref = blk * w + w // 2 - 1
        rows.append(tri[ref])
        m = (blk[:, None] == blk[None, :]) & (pos[:, None] >= w // 2) & (pos[None, :] < w // 2)
        lvl[m] = float(l)
    w_f = np.concatenate(rows, axis=0)
    w_b = w_f.reshape(LEVELS + 1, c, c)[:, ::-1, ::-1].reshape(-1, c)
    heads = GROUP // HEAD_DIM
    head_mask = np.kron(np.eye(heads, dtype=np.float32), np.ones((c, HEAD_DIM), np.float32))
    return (np.tile(w_f, (1, 3)), np.tile(w_b, (1, 3)),
            np.tile(lvl, (1, heads)), np.tile(lvl.T, (1, heads)), head_mask)


HGRN_FAST_RANGE = 75.0


def _hgrn_gates(z, lb):
    e = jnp.exp(-jnp.abs(z))
    r = 1.0 / (1.0 + e)
    sig_pos = jnp.where(z >= 0, r, e * r)
    sig_neg = jnp.where(z >= 0, e * r, r)
    return jnp.log(lb + (1.0 - lb) * sig_pos), (1.0 - lb) * sig_neg


def _split3(g):
    g_hi = g.astype(BF16)
    r1 = g - g_hi.astype(F32)
    g_mid = r1.astype(BF16)
    g_lo = (r1 - g_mid.astype(F32)).astype(BF16)
    return jnp.concatenate([g_hi, g_mid, g_lo], axis=0)


def _block_diag(x, hm_b):
    return jnp.concatenate([x.astype(BF16)] * (GROUP // HEAD_DIM), axis=0) * hm_b


def _hgrn_output(ch, a, hm, hm_b, st_ref, end_row):
    q, k, v, b = ch["q"], ch["k"], ch["v"], ch["b"]
    b_end = b[end_row:end_row + 1]
    st = st_ref[...]
    o = _dot(a.astype(BF16), _block_diag(v, hm_b))
    o = o + _dot_nt((q * jnp.exp(b)).astype(BF16), st.astype(BF16))
    upd = _dot_tn(v.astype(BF16), (k * jnp.exp(b_end - b)).astype(BF16))
    st_ref[...] = st * jnp.exp(b_end) + upd * hm
    return o


def _hgrn_scores_fast(ch, lvl, hm_b):
    d = ch["b"] - ch["b_mid"]
    a = _dot_nt((ch["q"] * jnp.exp(d)).astype(BF16), _block_diag(ch["k"] * jnp.exp(-d), hm_b))
    return jnp.where(lvl >= 0.0, a, 0.0)


def _hgrn_scores_levels(ch, w3, lvl, hm_b):
    c = CHUNK
    q, k = ch["q"], ch["k"]
    sums = _dot(w3, ch["g3"])
    b = sums[0:c]
    a = jnp.where(lvl == 0.0, _dot_nt(q.astype(BF16), _block_diag(k, hm_b)), 0.0)
    for l in range(1, LEVELS + 1):
        d = jnp.exp(-jnp.abs(b - sums[l * c:(l + 1) * c]))
        a = jnp.where(lvl == float(l), _dot_nt((q * d).astype(BF16), _block_diag(k * d, hm_b)), a)
    return a


def _hgrn_kernel(qf_ref, zf_ref, vf_ref, qb_ref, zb_ref, vb_ref, lbf_ref, lbb_ref,
                 w3f_ref, w3b_ref, lvf_ref, lvb_ref, hm_ref, of_ref, ob_ref, sf_ref, sb_ref):
    @pl.when(pl.program_id(1) == 0)
    def _():
        sf_ref[...] = jnp.zeros_like(sf_ref)
        sb_ref[...] = jnp.zeros_like(sb_ref)

    hm = hm_ref[...]
    hm_b = hm.astype(BF16)
    n = TILE // CHUNK
    fwd = dict(q=qf_ref, z=zf_ref, v=vf_ref, lb=lbf_ref, w3=w3f_ref, lvl=lvf_ref, o=of_ref, st=sf_ref,
               end=CHUNK - 1)
    bwd = dict(q=qb_ref, z=zb_ref, v=vb_ref, lb=lbb_ref, w3=w3b_ref, lvl=lvb_ref, o=ob_ref, st=sb_ref,
               end=0)
    order = [(d, ci if d is fwd else n - 1 - ci) for ci in range(n) for d in (fwd, bwd)]

    chunks = []
    spread = None
    for d, ci in order:
        rows = slice(ci * CHUNK, (ci + 1) * CHUNK)
        g, k = _hgrn_gates(d["z"][0, rows, :], d["lb"][...])
        g3 = _split3(g)
        sums = _dot(d["w3"][0:2 * CHUNK, :], g3)
        ch = dict(q=d["q"][0, rows, :], v=d["v"][0, rows, :], k=k, g3=g3,
                  b=sums[0:CHUNK], b_mid=sums[CHUNK:2 * CHUNK])
        dist = jnp.abs(ch["b"] - ch["b_mid"])
        spread = dist if spread is None else jnp.maximum(spread, dist)
        chunks.append(ch)
    fast = jnp.max(spread) < HGRN_FAST_RANGE

    @pl.when(fast)
    def _():
        for (d, ci), ch in zip(order, chunks):
            a = _hgrn_scores_fast(ch, d["lvl"][...], hm_b)
            d["o"][0, ci * CHUNK:(ci + 1) * CHUNK, :] = _hgrn_output(ch, a, hm, hm_b, d["st"], d["end"])

    @pl.when(jnp.logical_not(fast))
    def _():
        for (d, ci), ch in zip(order, chunks):
            a = _hgrn_scores_levels(ch, d["w3"][...], d["lvl"][...], hm_b)
            d["o"][0, ci * CHUNK:(ci + 1) * CHUNK, :] = _hgrn_output(ch, a, hm, hm_b, d["st"], d["end"])


def _hgrn(proj, lb_f, lb_b, tables):
    b, t, _ = proj.shape
    nt = t // TILE
    w3f, w3b, lvf, lvb, hm = tables

    def fwd(col):
        return pl.BlockSpec((1, TILE, GROUP),
                            lambda i, j: (i, jnp.where(j == 0, nt - 1, j - 1), col // GROUP))

    def bwd(col):
        return pl.BlockSpec((1, TILE, GROUP), lambda i, j: (i, nt - 1 - j, col // GROUP))

    def const(a):
        return pl.BlockSpec(a.shape, lambda i, j: (0,) * a.ndim)

    out_f = fwd(0)
    out_b = bwd(0)
    lb_f = lb_f.reshape(1, GROUP)
    lb_b = lb_b.reshape(1, GROUP)
    return pl.pallas_call(
        _hgrn_kernel,
        grid=(b, nt),
        in_specs=[fwd(COL_B_Q), fwd(COL_B_FF), fwd(COL_B_I), bwd(COL_B_Q), bwd(COL_B_FB), bwd(COL_B_I),
                  const(lb_f), const(lb_b), const(w3f), const(w3b), const(lvf), const(lvb), const(hm)],
        out_specs=[out_f, out_b],
        out_shape=[jax.ShapeDtypeStruct((b, t, GROUP), F32)] * 2,
        scratch_shapes=[pltpu.VMEM((GROUP, GROUP), F32), pltpu.VMEM((GROUP, GROUP), F32)],
        compiler_params=_cparams(2),
        name="hgrn2",
    )(proj, proj, proj, proj, proj, proj, lb_f, lb_b, w3f, w3b, lvf, lvb, hm)


def _outffn_kernel(x_ref, ma_ref, of_ref, ob_ref, g_ref, mc_ref, md_ref, mod_ref, hg_ref, n2_ref,
                   fn_ref, bd_ref, wo_ref, wi_ref, wd_ref, o_ref, act_ref, *, hidden, final):
    n_sub = x_ref.shape[1] // TILE
    o = of_ref[0] + ob_ref[0]
    g = g_ref[0]
    mb = (o * lax.rsqrt(_group_mean_sq(o, bd_ref[...]) + RMS_EPS) * hg_ref[...]) * (g * jax.nn.sigmoid(g))
    mix = jnp.concatenate([ma_ref[0], mb, mc_ref[0], md_ref[0]], axis=-1).astype(BF16)
    y = _dot(mix, wo_ref[...])
    xs, hs, gates = [], [], []
    for i in range(n_sub):
        m = _mod_rows(mod_ref, i, n_sub, not final)
        x = x_ref[0, i * TILE:(i + 1) * TILE, :] + m[2:3] * y[i * TILE:(i + 1) * TILE]
        ms = jnp.mean(x * x, axis=-1, keepdims=True)
        xs.append(x)
        hs.append((x * lax.rsqrt(ms + RMS_EPS) * (n2_ref[...] * (1.0 + m[4:5])) + m[3:4]).astype(BF16))
        gates.append(m[5:6])
    h = jnp.concatenate(hs, axis=0)
    step = 256
    for c in range(hidden // step):
        gate = _dot(h, wi_ref[:, c * step:(c + 1) * step])
        up = _dot(h, wi_ref[:, hidden + c * step:hidden + (c + 1) * step])
        act_ref[:, c * step:(c + 1) * step] = (gate * jax.nn.sigmoid(gate) * up).astype(BF16)
    acc = _dot(act_ref[...], wd_ref[...])
    for i in range(n_sub):
        x = xs[i] + gates[i] * acc[i * TILE:(i + 1) * TILE]
        if final:
            ms = jnp.mean(x * x, axis=-1, keepdims=True)
            x = x * lax.rsqrt(ms + RMS_EPS) * fn_ref[...]
        o_ref[0, i * TILE:(i + 1) * TILE, :] = x


def _out_ffn(xs, ma, o_f, o_b, proj, mc, md, modsel, hgrn_gain, norm2, final_norm, bd, w_out,
             w_ffn_in, w_ffn_out, layer, n_rows, final):
    b, _, d = xs.shape
    hidden = w_ffn_out.shape[1]
    tm = _row_tile(n_rows) if not final else 2 * TILE

    def rows(width, col=0):
        return pl.BlockSpec((1, tm, width), lambda i, j: (i, j, col))

    hg = jnp.tile(hgrn_gain, GROUP // HEAD_DIM).reshape(1, GROUP)
    return pl.pallas_call(
        functools.partial(_outffn_kernel, hidden=hidden, final=final),
        grid=(b, n_rows // tm),
        in_specs=[rows(d), rows(GROUP), rows(GROUP), rows(GROUP), rows(GROUP, COL_B_G // GROUP),
                  rows(GROUP), rows(GROUP),
                  pl.BlockSpec((1, 2, 6, d), lambda i, j: (i, 0, 0, 0)),
                  _resident((1, GROUP)), _resident((1, d)), _resident((1, d)),
                  _resident((GROUP, GROUP)), _layer_resident(w_out, layer),
                  _layer_resident(w_ffn_in, layer), _layer_resident(w_ffn_out, layer)],
        out_specs=rows(d),
        out_shape=jax.ShapeDtypeStruct((b, n_rows, d), F32),
        scratch_shapes=[pltpu.VMEM((tm, hidden), BF16)],
        compiler_params=_cparams(2),
        name="out_ffn",
    )(xs, ma, o_f, o_b, proj, mc, md, modsel, hg, norm2.reshape(1, d), final_norm.reshape(1, d), bd,
      w_out, w_ffn_in, w_ffn_out)


def _rope_tables(tc, s, dim, width):
    n_freq = dim // 4
    inv_freq = ROPE_THETA ** (-jnp.arange(n_freq, dtype=F32) / n_freq)
    lane = np.arange(width) % dim
    is_col = (lane // (dim // 2)) == 1
    second = ((lane % (dim // 2)) // n_freq) == 1
    freq = inv_freq[lane % n_freq]
    ang_r = jnp.arange(s // GRID_W, dtype=F32)[:, None] * freq[None, :]
    ang_c = jnp.arange(GRID_W, dtype=F32)[:, None] * freq[None, :]

    def on_grid(f):
        return jnp.where(is_col[None, None, :], f(ang_c)[None, :, :], f(ang_r)[:, None, :]).reshape(s, width)

    cos, sin = on_grid(jnp.cos), on_grid(jnp.sin)
    sin_a = jnp.where(second[None, :], sin, 0.0)
    sin_b = jnp.where(second[None, :], 0.0, -sin)
    ident = jnp.ones((tc, width), F32)
    zero = jnp.zeros((tc, width), F32)
    return (jnp.concatenate([cos, ident], axis=0), jnp.concatenate([sin_a, zero], axis=0),
            jnp.concatenate([sin_b, zero], axis=0))


def kernel(x, c, ctx, c_ctx, w_mod, b_mod, norm1, w_in, diff_lambda, diff_norm, hgrn_lb_logits,
           hgrn_norm, q_norm, k_norm, w_out, norm2, w_ffn_in, w_ffn_out, final_norm):
    b, s, d = x.shape
    tc = ctx.shape[1]
    depth = w_mod.shape[0]
    assert tc == TILE and s % TILE == 0 and s % GRID_W == 0 and d == 4 * GROUP

    rope_a = _rope_tables(tc, s, A_QK, GROUP)
    rope_c = _rope_tables(tc, s, HEAD_DIM, GROUP)
    dft = _dft_tables(tc, s)
    hgrn_tabs = _hgrn_tables()
    hgrn_tabs = tuple(jnp.asarray(a, BF16 if i < 2 else F32) for i, a in enumerate(hgrn_tabs))
    bd = jnp.asarray(np.kron(np.eye(GROUP // HEAD_DIM), np.full((HEAD_DIM, HEAD_DIM), 1.0 / HEAD_DIM)), BF16)

    p = jax.nn.softmax(hgrn_lb_logits.astype(F32), axis=1)
    lower = jnp.cumsum(p, axis=1) - p[:, :1]

    def swap_heads(w, axis, start):
        cuts = [0, start + HEAD_DIM, start + 2 * HEAD_DIM, start + 3 * HEAD_DIM, w.shape[axis]]
        parts = [lax.slice_in_dim(w, cuts[i], cuts[i + 1], axis=axis) for i in range(4)]
        return jnp.concatenate([parts[0], parts[2], parts[1], parts[3]], axis=axis).astype(BF16)

    w_in_b = swap_heads(w_in, 2, COL_C_Q)
    w_out_b = swap_heads(w_out, 1, 2 * GROUP)
    w_ffn_in_b = w_ffn_in.astype(BF16)
    w_ffn_out_b = w_ffn_out.astype(BF16)

    rows = -(-(b + 1) // 8) * 8
    cc = jnp.zeros((rows, d), F32).at[:b].set(c).at[b].set(c_ctx)
    mod = _modulation(cc, w_mod, b_mod).reshape(depth, rows, 6, d)
    modsel = jnp.stack([jnp.broadcast_to(mod[:, b:b + 1], (depth, b, 6, d)), mod[:, :b]], axis=2)

    xs = jnp.concatenate([x, ctx], axis=1)
    for l in range(depth):
        last = l == depth - 1
        lam_init = 0.8 - 0.6 * math.exp(-0.3 * l)
        proj = _in_proj(xs, modsel[l], norm1[l], w_in_b, l)
        m_a, m_c = _attention(proj, rope_a, rope_c, diff_lambda[l], diff_norm[l], q_norm[l], k_norm[l],
                              bd, tc, lam_init)
        o_f, o_b = _hgrn(proj, lower[0, l], lower[1, l], hgrn_tabs)
        m_d = _fourier(proj, *dft, tc)
        xs = _out_ffn(xs, m_a, o_f, o_b, proj, m_c, m_d, modsel[l], hgrn_norm[l], norm2[l], final_norm,
                      bd, w_out_b, w_ffn_in_b, w_ffn_out_b, l, s if last else s + tc, last)
    return xs
```

```python
import functools
import math

import numpy as np
import jax
import jax.numpy as jnp
from jax import lax
from jax.experimental import pallas as pl
from jax.experimental.pallas import tpu as pltpu

F32 = jnp.float32
BF16 = jnp.bfloat16

HEAD_DIM = 64
GRID_W = 64
ROPE_THETA = 10000.0
RMS_EPS = 1e-6
GROUP = 256
A_QK = 32
TILE = 256
CHUNK = 64
LEVELS = 6
LOG2E = 1.4426950408889634
VMEM_LIMIT = 56 * 1024 * 1024

COL_A_Q, COL_A_K, COL_A_V = 0, 256, 512
COL_B_Q, COL_B_FF, COL_B_FB, COL_B_I, COL_B_G = 768, 1024, 1280, 1536, 1792
COL_C_Q, COL_C_K, COL_C_V = 2048, 2304, 2432
COL_D_U = 2560


def _cparams(n_axes):
    return pltpu.CompilerParams(dimension_semantics=("arbitrary",) * n_axes,
                                vmem_limit_bytes=VMEM_LIMIT)


def _dot(a, b):
    return jnp.dot(a, b, preferred_element_type=F32)


def _dot_nt(a, b):
    return lax.dot_general(a, b, (((1,), (1,)), ((), ())), preferred_element_type=F32)


def _dot_tn(a, b):
    return lax.dot_general(a, b, (((0,), (0,)), ((), ())), preferred_element_type=F32)


def _group_mean_sq(y, bd):
    y2 = y * y
    hi = y2.astype(BF16)
    lo = (y2 - hi.astype(F32)).astype(BF16)
    return _dot(hi, bd) + _dot(lo, bd)


def _rope(x, cos, sin_a, sin_b, shift):
    w = x.shape[-1]
    return x * cos + pltpu.roll(x, shift, 1) * sin_a + pltpu.roll(x, w - shift, 1) * sin_b


def _mod_kernel(c_ref, w_ref, b_ref, o_ref):
    c = c_ref[...]
    a = c * jax.nn.sigmoid(c)
    o_ref[0] = _dot(a.astype(BF16), w_ref[0].astype(BF16)) + b_ref[0]


def _modulation(cc, w_mod, b_mod):
    depth, d, n = w_mod.shape
    r = cc.shape[0]
    tn = 1536
    return pl.pallas_call(
        _mod_kernel,
        grid=(depth, n // tn),
        in_specs=[pl.BlockSpec((r, d), lambda l, j: (0, 0)),
                  pl.BlockSpec((1, d, tn), lambda l, j: (l, 0, j)),
                  pl.BlockSpec((1, 1, tn), lambda l, j: (l, 0, j))],
        out_specs=pl.BlockSpec((1, r, tn), lambda l, j: (l, 0, j)),
        out_shape=jax.ShapeDtypeStruct((depth, r, n), F32),
        compiler_params=_cparams(2),
        name="modulation",
    )(cc, w_mod, b_mod.reshape(depth, 1, n))


def _mod_rows(mod_ref, sub, n_sub, with_ctx):
    m = mod_ref[0]
    if not (with_ctx and sub == n_sub - 1):
        return m[1]
    return jnp.where(pl.program_id(1) == pl.num_programs(1) - 1, m[0], m[1])


def _stream_rows(x_ref, ctx_ref, sub, n_sub):
    rows = slice(sub * TILE, (sub + 1) * TILE)
    if ctx_ref is None or sub < n_sub - 1:
        return x_ref[0, rows, :]
    return lax.cond(pl.program_id(1) == pl.num_programs(1) - 1,
                    lambda: ctx_ref[0], lambda: x_ref[0, rows, :])


def _inproj_kernel(*refs, split_input):
    x_ref, ctx_ref = (refs[0], refs[1]) if split_input else (refs[0], None)
    mod_ref, n1_ref, w_ref, o_ref = refs[-4:]
    n_sub = x_ref.shape[1] // TILE
    hs = []
    for i in range(n_sub):
        x = _stream_rows(x_ref, ctx_ref, i, n_sub)
        ms = jnp.mean(x * x, axis=-1, keepdims=True)
        y = x * lax.rsqrt(ms + RMS_EPS) * n1_ref[...]
        m = _mod_rows(mod_ref, i, n_sub, True)
        hs.append((y * (1.0 + m[1:2]) + m[0:1]).astype(BF16))
    o_ref[0] = _dot(jnp.concatenate(hs, axis=0), w_ref[...])


def _row_tile(t):
    return next(tm for tm in (3 * TILE, 2 * TILE, TILE) if t % tm == 0)


def _resident(shape):
    return pl.BlockSpec(shape, lambda i, j: (0,) * len(shape), pipeline_mode=pl.Buffered(1))


def _layer_resident(w, layer):
    return pl.BlockSpec((None,) + w.shape[1:], lambda i, j: (layer, 0, 0), pipeline_mode=pl.Buffered(1))


def _stream_specs(stream, t, tm):
    if not isinstance(stream, tuple):
        return (stream,), [pl.BlockSpec((1, tm, stream.shape[2]), lambda i, j: (i, j, 0))]
    x, ctx = stream
    assert x.shape[1] + ctx.shape[1] == t and ctx.shape[1] == TILE and t % tm == 0
    return stream, [pl.BlockSpec((1, tm, x.shape[2]), lambda i, j: (i, j, 0)),
                    pl.BlockSpec((1, TILE, x.shape[2]), lambda i, j: (i, 0, 0))]


def _in_proj(stream, t, modsel, norm1, w_in, layer):
    b, d = modsel.shape[0], modsel.shape[3]
    n = w_in.shape[2]
    tm = _row_tile(t)
    operands, specs = _stream_specs(stream, t, tm)
    return pl.pallas_call(
        functools.partial(_inproj_kernel, split_input=isinstance(stream, tuple)),
        grid=(b, t // tm),
        in_specs=specs + [pl.BlockSpec((1, 2, 6, d), lambda i, j: (i, 0, 0, 0)),
                          _resident((1, d)), _layer_resident(w_in, layer)],
        out_specs=pl.BlockSpec((1, tm, n), lambda i, j: (i, j, 0)),
        out_shape=jax.ShapeDtypeStruct((b, t, n), F32),
        compiler_params=_cparams(2),
        name="in_proj",
    )(*operands, modsel, norm1.reshape(1, d), w_in)


V_ROWS = HEAD_DIM + 16


KEY_TILES = (768, 256)


def _attend_t(groups, s_ref, lo, nk):
    key_tile = next(c for c in KEY_TILES if nk % c == 0)
    n_kt = nk // key_tile
    n_g = len(groups)
    maxes = [None] * n_g
    accs = [None] * n_g

    def keys(kt):
        return slice(lo + kt * key_tile, lo + (kt + 1) * key_tile)

    def score_tile(g, kt):
        kr_ref, wq, _, _ = groups[g]
        s = _dot_nt(kr_ref[keys(kt), :], wq)
        s_ref[g % 2, kt * key_tile:(kt + 1) * key_tile, :] = s
        m = jnp.max(s, axis=0, keepdims=True)
        maxes[g] = m if kt == 0 else jnp.maximum(maxes[g], m)

    def value_tile(g, kt, p):
        _, _, vt_ref, head = groups[g]
        pv = _dot(vt_ref[head, :, keys(kt)], p)
        accs[g] = pv if kt == 0 else accs[g] + pv

    for kt in range(n_kt):
        score_tile(0, kt)
    for g in range(n_g):
        p_prev = None
        for kt in range(n_kt):
            if g + 1 < n_g:
                score_tile(g + 1, kt)
            p = jnp.exp2(s_ref[g % 2, kt * key_tile:(kt + 1) * key_tile, :] - maxes[g]).astype(BF16)
            if kt > 0:
                value_tile(g, kt - 1, p_prev)
            p_prev = p
        value_tile(g, n_kt - 1, p_prev)
    return [a[0:HEAD_DIM] * (1.0 / a[HEAD_DIM:HEAD_DIM + 1]) for a in accs]


def _store_values_t(vt_ref, v):
    v_t = v.T
    for h in range(vt_ref.shape[0]):
        vt_ref[h, 0:HEAD_DIM, :] = v_t[h * HEAD_DIM:(h + 1) * HEAD_DIM, :].astype(BF16)
        vt_ref[h, HEAD_DIM:V_ROWS, :] = jnp.ones((V_ROWS - HEAD_DIM, v.shape[0]), BF16)


def _attn_kernel(aq_ref, ak_ref, av_ref, cq_ref, ck_ref, cv_ref,
                 a_cos_q, a_sa_q, a_sb_q, a_cos_k, a_sa_k, a_sb_k,
                 c_cos_q, c_sa_q, c_sb_q, c_cos_k, c_sa_k, c_sb_k,
                 lp_ref, gain_ref, qg_ref, kg_ref, bd_ref, oa_ref, oc_ref,
                 akr_ref, avt_ref, ckr_ref, cvt_ref, s_ref, *, tc, lam_init):
    t = pl.program_id(1)
    n_tok = akr_ref.shape[0]
    half = GROUP // 2
    bd = bd_ref[...]

    @pl.when(t == 0)
    def _():
        kr = _rope(ak_ref[0], a_cos_k[...], a_sa_k[...], a_sb_k[...], A_QK // 4)
        akr_ref[...] = kr.astype(BF16)
        _store_values_t(avt_ref, av_ref[0])
        k = ck_ref[0]
        kn = k * lax.rsqrt(_group_mean_sq(k, bd[0:half, 0:half]) + RMS_EPS) * kg_ref[...]
        ckr_ref[...] = _rope(kn, c_cos_k[...], c_sa_k[...], c_sb_k[...], HEAD_DIM // 4).astype(BF16)
        _store_values_t(cvt_ref, cv_ref[0])

    lp = lp_ref[...]
    lam = (jnp.exp(jnp.sum(lp[0:1] * lp[1:2], axis=-1, keepdims=True))
           - jnp.exp(jnp.sum(lp[2:3] * lp[3:4], axis=-1, keepdims=True)) + lam_init)
    lane = lax.broadcasted_iota(jnp.int32, (1, GROUP), 1)
    lane_h = lax.broadcasted_iota(jnp.int32, (1, half), 1)
    n_heads = GROUP // HEAD_DIM

    def attend(lo, nk, n_sub):
        groups = []
        for u in range(n_sub):
            r = slice(u * TILE, (u + 1) * TILE)
            qa = _rope(aq_ref[0, r, :], a_cos_q[r, :], a_sa_q[r, :], a_sb_q[r, :], A_QK // 4)
            qa = qa * (A_QK ** -0.5 * LOG2E)
            q = cq_ref[0, r, :]
            qn = q * lax.rsqrt(_group_mean_sq(q, bd) + RMS_EPS) * qg_ref[...]
            qc = _rope(qn, c_cos_q[r, :], c_sa_q[r, :], c_sb_q[r, :], HEAD_DIM // 4)
            qc = qc * (HEAD_DIM ** -0.5 * LOG2E)
            groups += [(akr_ref, jnp.concatenate([jnp.where(lane // A_QK == 2 * h + c, qa, 0.0).astype(BF16)
                                                  for c in range(2)], axis=0), avt_ref, h)
                       for h in range(n_heads)]
            groups += [(ckr_ref, jnp.concatenate(
                [jnp.where(lane_h // HEAD_DIM == g, qc[:, qb * half:(qb + 1) * half], 0.0).astype(BF16)
                 for qb in range(2)], axis=0), cvt_ref, g) for g in range(2)]
        outs = _attend_t(groups, s_ref, lo, nk)
        per_sub = n_heads + 2
        for u in range(n_sub):
            r = slice(u * TILE, (u + 1) * TILE)
            o = outs[u * per_sub:(u + 1) * per_sub]
            heads = [x[:, 0:TILE] - lam * x[:, TILE:2 * TILE] for x in o[0:n_heads]]
            acc = jnp.concatenate(heads, axis=0).T
            y = acc * lax.rsqrt(_group_mean_sq(acc, bd) + RMS_EPS)
            oa_ref[0, r, :] = y * gain_ref[...] * (1.0 - lam_init)
            heads = [o[n_heads + g][:, qb * TILE:(qb + 1) * TILE] for qb in range(2) for g in range(2)]
            oc_ref[0, r, :] = jnp.concatenate(heads, axis=0).T

    @pl.when(t == pl.num_programs(1) - 1)
    def _():
        attend(n_tok - tc, tc, 1)

    @pl.when(t < pl.num_programs(1) - 1)
    def _():
        attend(0, n_tok, aq_ref.shape[1] // TILE)


def _attention(proj, rope_a, rope_c, lam_params, sub_gain, q_gain, k_gain, bd, tc, lam_init):
    b, t, _ = proj.shape
    half = GROUP // 2
    gain = jnp.tile(sub_gain, GROUP // HEAD_DIM).reshape(1, GROUP)
    qg = jnp.tile(q_gain, GROUP // HEAD_DIM).reshape(1, GROUP)
    kg = jnp.tile(k_gain, half // HEAD_DIM).reshape(1, half)

    tq = 2 * TILE
    assert (t - tc) % tq == 0 and tc == TILE

    def rows(width, col):
        return pl.BlockSpec((1, tq, width), lambda i, j: (i, j, col // width))

    def sample(width, col):
        return pl.BlockSpec((1, t, width), lambda i, j: (i, 0, col // width))

    tile_spec = pl.BlockSpec((tq, GROUP), lambda i, j: (j, 0))
    out_spec = pl.BlockSpec((1, tq, GROUP), lambda i, j: (i, j, 0))
    return pl.pallas_call(
        functools.partial(_attn_kernel, tc=tc, lam_init=lam_init),
        grid=(b, (t - tc) // tq + 1),
        in_specs=[rows(GROUP, COL_A_Q), sample(GROUP, COL_A_K), sample(GROUP, COL_A_V),
                  rows(GROUP, COL_C_Q), sample(half, COL_C_K), sample(half, COL_C_V),
                  tile_spec, tile_spec, tile_spec,
                  _resident((t, GROUP)), _resident((t, GROUP)), _resident((t, GROUP)),
                  tile_spec, tile_spec, tile_spec,
                  _resident((t, half)), _resident((t, half)), _resident((t, half)),
                  _resident(lam_params.shape), _resident((1, GROUP)), _resident((1, GROUP)),
                  _resident((1, half)), _resident((GROUP, GROUP))],
        out_specs=[out_spec, out_spec],
        out_shape=[jax.ShapeDtypeStruct((b, t, GROUP), F32)] * 2,
        scratch_shapes=[pltpu.VMEM((t, GROUP), BF16),
                        pltpu.VMEM((GROUP // HEAD_DIM, V_ROWS, t), BF16),
                        pltpu.VMEM((t, half), BF16),
                        pltpu.VMEM((half // HEAD_DIM, V_ROWS, t), BF16),
                        pltpu.VMEM((2, t, 2 * TILE), F32)],
        compiler_params=_cparams(2),
        name="attention",
    )(proj, proj, proj, proj, proj, proj, *rope_a, *rope_a, *rope_c, *rope_c,
      lam_params, gain, qg, kg, bd)


def _fourier_stage_kernel(ux_ref, uc_ref, c64_ref, dctx_ref, w_ref, o_ref):
    s = ux_ref.shape[1]
    tc = uc_ref.shape[1]
    w = _dot(ux_ref[0].astype(BF16), c64_ref[...])
    w_ref[0:s, :] = w[:, 0:GROUP].astype(BF16)
    w_ref[s:2 * s, :] = w[:, GROUP:2 * GROUP].astype(BF16)
    wc = _dot(uc_ref[0].astype(BF16), c64_ref[...])
    wc = jnp.concatenate([wc[:, 0:GROUP], wc[:, GROUP:2 * GROUP]], axis=0).astype(BF16)
    o_ref[0] = _dot(dctx_ref[...], wc) * (tc ** -0.5)


def _fourier_dft_kernel(d_ref, w_ref, alias_ref, o_ref):
    del alias_ref
    o_ref[0] = _dot(d_ref[...], w_ref[...]) * ((d_ref.shape[1] // 2) ** -0.5)


def _fourier(proj, c64, dctx, dx, tc):
    b, t, _ = proj.shape
    s = t - tc
    col = COL_D_U // GROUP
    w_all, o_ctx = pl.pallas_call(
        _fourier_stage_kernel,
        grid=(b,),
        in_specs=[pl.BlockSpec((1, s, GROUP), lambda i: (i, 0, col)),
                  pl.BlockSpec((1, tc, GROUP), lambda i: (i, s // tc, col)),
                  pl.BlockSpec((GROUP, 2 * GROUP), lambda i: (0, 0)),
                  pl.BlockSpec((tc, 2 * tc), lambda i: (0, 0))],
        out_specs=[pl.BlockSpec((2 * s, GROUP), lambda i: (0, i)),
                   pl.BlockSpec((1, tc, GROUP), lambda i: (i, s // tc, 0))],
        out_shape=[jax.ShapeDtypeStruct((2 * s, b * GROUP), BF16),
                   jax.ShapeDtypeStruct((b, t, GROUP), F32)],
        compiler_params=_cparams(1),
        name="fourier_stage",
    )(proj, proj, c64, dctx)
    tm = next(c for c in (4 * TILE, 2 * TILE) if s % c == 0)
    return pl.pallas_call(
        _fourier_dft_kernel,
        grid=(s // tm, b),
        in_specs=[pl.BlockSpec((tm, 2 * s), lambda i, j: (i, 0)),
                  pl.BlockSpec((2 * s, GROUP), lambda i, j: (0, j)),
                  pl.BlockSpec(memory_space=pl.ANY)],
        out_specs=pl.BlockSpec((1, tm, GROUP), lambda i, j: (j, i, 0)),
        out_shape=jax.ShapeDtypeStruct((b, t, GROUP), F32),
        input_output_aliases={2: 0},
        compiler_params=_cparams(2),
        name="fourier_dft",
    )(dx, w_all, o_ctx)


def _dft_tables(tc, s):
    def cs_direct(rows, n, period, quarter_turn=False):
        j = jnp.arange(rows, dtype=jnp.int32)
        k = jnp.arange(n, dtype=jnp.int32)
        ang = ((j[:, None] * k[None, :]) % period).astype(F32) * (2.0 * math.pi / period)
        if quarter_turn:
            ang = jnp.concatenate([ang, ang + 0.5 * math.pi], axis=1)
        return jnp.cos(ang), jnp.sin(ang)

    def cos_neg_sin(n):
        ca, sa = cs_direct(n // GRID_W, n, n // GRID_W)
        ca, sa = jnp.tile(ca, (1, 2)), jnp.tile(sa, (1, 2))
        cb, sb = cs_direct(GRID_W, n, n, quarter_turn=True)
        cos = ca[:, None, :] * cb[None, :, :] - sa[:, None, :] * sb[None, :, :]
        return cos.reshape(n, 2 * n).astype(BF16)

    c64, s64 = cs_direct(HEAD_DIM, HEAD_DIM, HEAD_DIM)
    eye = jnp.eye(GROUP // HEAD_DIM, dtype=F32)
    c64 = jnp.concatenate([jnp.kron(eye, c64), jnp.kron(eye, s64)], axis=1) * (HEAD_DIM ** -0.5)
    return c64.astype(BF16), cos_neg_sin(tc), cos_neg_sin(s)


def _hgrn_tables():
    c = CHUNK
    idx = np.arange(c)
    tri = (idx[None, :] <= idx[:, None]).astype(np.float32)
    lvl = np.full((c, c), -1.0, np.float32)
    lvl[idx, idx] = 0.0
    rows = [tri]
    for l in range(1, LEVELS + 1):
        w = c >> (l - 1)
        blk, pos = idx // w, idx % w
        ref = blk * w + w // 2 - 1
        rows.append(tri[ref])
        m = (blk[:, None] == blk[None, :]) & (pos[:, None] >= w // 2) & (pos[None, :] < w // 2)
        lvl[m] = float(l)
    w_f = np.concatenate(rows, axis=0)
    w_b = w_f.reshape(LEVELS + 1, c, c)[:, ::-1, ::-1].reshape(-1, c)
    heads = GROUP // HEAD_DIM
    head_mask = np.kron(np.eye(heads, dtype=np.float32), np.ones((c, HEAD_DIM), np.float32))
    return (np.tile(w_f, (1, 3)), np.tile(w_b, (1, 3)),
            np.tile(lvl, (1, heads)), np.tile(lvl.T, (1, heads)), head_mask)


HGRN_FAST_RANGE = 75.0


def _hgrn_gates(z, lb):
    e = jnp.exp(-jnp.abs(z))
    r = 1.0 / (1.0 + e)
    sig_pos = jnp.where(z >= 0, r, e * r)
    sig_neg = jnp.where(z >= 0, e * r, r)
    return jnp.log(lb + (1.0 - lb) * sig_pos), (1.0 - lb) * sig_neg


def _split3(g):
    g_hi = g.astype(BF16)
    r1 = g - g_hi.astype(F32)
    g_mid = r1.astype(BF16)
    g_lo = (r1 - g_mid.astype(F32)).astype(BF16)
    return jnp.concatenate([g_hi, g_mid, g_lo], axis=0)


def _block_diag(x, hm_b):
    return jnp.concatenate([x.astype(BF16)] * (GROUP // HEAD_DIM), axis=0) * hm_b


def _hgrn_output(ch, a, hm, hm_b, st_ref, end_row):
    q, k, v, b = ch["q"], ch["k"], ch["v"], ch["b"]
    b_end = b[end_row:end_row + 1]
    st = st_ref[...]
    o = _dot(a.astype(BF16), _block_diag(v, hm_b))
    o = o + _dot_nt((q * jnp.exp(b)).astype(BF16), st.astype(BF16))
    upd = _dot_tn(v.astype(BF16), (k * jnp.exp(b_end - b)).astype(BF16))
    st_ref[...] = st * jnp.exp(b_end) + upd * hm
    return o


def _hgrn_scores_fast(ch, lvl, hm_b):
    d = ch["b"] - ch["b_mid"]
    a = _dot_nt((ch["q"] * jnp.exp(d)).astype(BF16), _block_diag(ch["k"] * jnp.exp(-d), hm_b))
    return jnp.where(lvl >= 0.0, a, 0.0)


def _hgrn_scores_levels(ch, w3, lvl, hm_b):
    c = CHUNK
    q, k = ch["q"], ch["k"]
    sums = _dot(w3, ch["g3"])
    b = sums[0:c]
    a = jnp.where(lvl == 0.0, _dot_nt(q.astype(BF16), _block_diag(k, hm_b)), 0.0)
    for l in range(1, LEVELS + 1):
        d = jnp.exp(-jnp.abs(b - sums[l * c:(l + 1) * c]))
        a = jnp.where(lvl == float(l), _dot_nt((q * d).astype(BF16), _block_diag(k * d, hm_b)), a)
    return a


def _hgrn_kernel(qf_ref, zf_ref, vf_ref, qb_ref, zb_ref, vb_ref, lbf_ref, lbb_ref,
                 w3f_ref, w3b_ref, lvf_ref, lvb_ref, hm_ref, of_ref, ob_ref, sf_ref, sb_ref):
    @pl.when(pl.program_id(1) == 0)
    def _():
        sf_ref[...] = jnp.zeros_like(sf_ref)
        sb_ref[...] = jnp.zeros_like(sb_ref)

    hm = hm_ref[...]
    hm_b = hm.astype(BF16)
    n = TILE // CHUNK
    fwd = dict(q=qf_ref, z=zf_ref, v=vf_ref, lb=lbf_ref, w3=w3f_ref, lvl=lvf_ref, o=of_ref, st=sf_ref,
               end=CHUNK - 1)
    bwd = dict(q=qb_ref, z=zb_ref, v=vb_ref, lb=lbb_ref, w3=w3b_ref, lvl=lvb_ref, o=ob_ref, st=sb_ref,
               end=0)
    order = [(d, ci if d is fwd else n - 1 - ci) for ci in range(n) for d in (fwd, bwd)]

    chunks = []
    spread = None
    for d, ci in order:
        rows = slice(ci * CHUNK, (ci + 1) * CHUNK)
        g, k = _hgrn_gates(d["z"][0, rows, :], d["lb"][...])
        g3 = _split3(g)
        sums = _dot(d["w3"][0:2 * CHUNK, :], g3)
        ch = dict(q=d["q"][0, rows, :], v=d["v"][0, rows, :], k=k, g3=g3,
                  b=sums[0:CHUNK], b_mid=sums[CHUNK:2 * CHUNK])
        dist = jnp.abs(ch["b"] - ch["b_mid"])
        spread = dist if spread is None else jnp.maximum(spread, dist)
        chunks.append(ch)
    fast = jnp.max(spread) < HGRN_FAST_RANGE

    @pl.when(fast)
    def _():
        for (d, ci), ch in zip(order, chunks):
            a = _hgrn_scores_fast(ch, d["lvl"][...], hm_b)
            d["o"][0, ci * CHUNK:(ci + 1) * CHUNK, :] = _hgrn_output(ch, a, hm, hm_b, d["st"], d["end"])

    @pl.when(jnp.logical_not(fast))
    def _():
        for (d, ci), ch in zip(order, chunks):
            a = _hgrn_scores_levels(ch, d["w3"][...], d["lvl"][...], hm_b)
            d["o"][0, ci * CHUNK:(ci + 1) * CHUNK, :] = _hgrn_output(ch, a, hm, hm_b, d["st"], d["end"])


def _hgrn(proj, lb_f, lb_b, tables):
    b, t, _ = proj.shape
    nt = t // TILE
    w3f, w3b, lvf, lvb, hm = tables

    def fwd(col):
        return pl.BlockSpec((1, TILE, GROUP),
                            lambda i, j: (i, jnp.where(j == 0, nt - 1, j - 1), col // GROUP))

    def bwd(col):
        return pl.BlockSpec((1, TILE, GROUP), lambda i, j: (i, nt - 1 - j, col // GROUP))

    def const(a):
        return pl.BlockSpec(a.shape, lambda i, j: (0,) * a.ndim)

    out_f = fwd(0)
    out_b = bwd(0)
    lb_f = lb_f.reshape(1, GROUP)
    lb_b = lb_b.reshape(1, GROUP)
    return pl.pallas_call(
        _hgrn_kernel,
        grid=(b, nt),
        in_specs=[fwd(COL_B_Q), fwd(COL_B_FF), fwd(COL_B_I), bwd(COL_B_Q), bwd(COL_B_FB), bwd(COL_B_I),
                  const(lb_f), const(lb_b), const(w3f), const(w3b), const(lvf), const(lvb), const(hm)],
        out_specs=[out_f, out_b],
        out_shape=[jax.ShapeDtypeStruct((b, t, GROUP), F32)] * 2,
        scratch_shapes=[pltpu.VMEM((GROUP, GROUP), F32), pltpu.VMEM((GROUP, GROUP), F32)],
        compiler_params=_cparams(2),
        name="hgrn2",
    )(proj, proj, proj, proj, proj, proj, lb_f, lb_b, w3f, w3b, lvf, lvb, hm)


def _outffn_kernel(*refs, hidden, final, split_input):
    x_ref, ctx_ref = (refs[0], refs[1]) if split_input else (refs[0], None)
    (ma_ref, of_ref, ob_ref, g_ref, mc_ref, md_ref, mod_ref, hg_ref, n2_ref, fn_ref, bd_ref,
     wo_ref, wi_ref, wd_ref, o_ref, act_ref) = refs[-16:]
    n_sub = x_ref.shape[1] // TILE
    o = of_ref[0] + ob_ref[0]
    g = g_ref[0]
    mb = (o * lax.rsqrt(_group_mean_sq(o, bd_ref[...]) + RMS_EPS) * hg_ref[...]) * (g * jax.nn.sigmoid(g))
    mix = jnp.concatenate([ma_ref[0], mb, mc_ref[0], md_ref[0]], axis=-1).astype(BF16)
    y = _dot(mix, wo_ref[...])
    xs, hs, gates = [], [], []
    for i in range(n_sub):
        m = _mod_rows(mod_ref, i, n_sub, not final)
        x = _stream_rows(x_ref, ctx_ref, i, n_sub) + m[2:3] * y[i * TILE:(i + 1) * TILE]
        ms = jnp.mean(x * x, axis=-1, keepdims=True)
        xs.append(x)
        hs.append((x * lax.rsqrt(ms + RMS_EPS) * (n2_ref[...] * (1.0 + m[4:5])) + m[3:4]).astype(BF16))
        gates.append(m[5:6])
    h = jnp.concatenate(hs, axis=0)
    step = 256
    for c in range(hidden // step):
        gate = _dot(h, wi_ref[:, c * step:(c + 1) * step])
        up = _dot(h, wi_ref[:, hidden + c * step:hidden + (c + 1) * step])
        act_ref[:, c * step:(c + 1) * step] = (gate * jax.nn.sigmoid(gate) * up).astype(BF16)
    acc = _dot(act_ref[...], wd_ref[...])
    for i in range(n_sub):
        x = xs[i] + gates[i] * acc[i * TILE:(i + 1) * TILE]
        if final:
            ms = jnp.mean(x * x, axis=-1, keepdims=True)
            x = x * lax.rsqrt(ms + RMS_EPS) * fn_ref[...]
        o_ref[0, i * TILE:(i + 1) * TILE, :] = x


def _out_ffn(xs, ma, o_f, o_b, proj, mc, md, modsel, hgrn_gain, norm2, final_norm, bd, w_out,
             w_ffn_in, w_ffn_out, layer, n_rows, final):
    b, d = modsel.shape[0], modsel.shape[3]
    hidden = w_ffn_out.shape[1]
    tm = _row_tile(n_rows) if not final else 2 * TILE

    def rows(width, col=0):
        return pl.BlockSpec((1, tm, width), lambda i, j: (i, j, col))

    hg = jnp.tile(hgrn_gain, GROUP // HEAD_DIM).reshape(1, GROUP)
    operands, specs = _stream_specs(xs, n_rows, tm)
    return pl.pallas_call(
        functools.partial(_outffn_kernel, hidden=hidden, final=final, split_input=isinstance(xs, tuple)),
        grid=(b, n_rows // tm),
        in_specs=specs + [rows(GROUP), rows(GROUP), rows(GROUP), rows(GROUP, COL_B_G // GROUP),
                          rows(GROUP), rows(GROUP),
                          pl.BlockSpec((1, 2, 6, d), lambda i, j: (i, 0, 0, 0)),
                          _resident((1, GROUP)), _resident((1, d)), _resident((1, d)),
                          _resident((GROUP, GROUP)), _layer_resident(w_out, layer),
                          _layer_resident(w_ffn_in, layer), _layer_resident(w_ffn_out, layer)],
        out_specs=rows(d),
        out_shape=jax.ShapeDtypeStruct((b, n_rows, d), F32),
        scratch_shapes=[pltpu.VMEM((tm, hidden), BF16)],
        compiler_params=_cparams(2),
        name="out_ffn",
    )(*operands, ma, o_f, o_b, proj, mc, md, modsel, hg, norm2.reshape(1, d), final_norm.reshape(1, d),
      bd, w_out, w_ffn_in, w_ffn_out)


def _rope_tables(tc, s, dim, width):
    n_freq = dim // 4
    inv_freq = ROPE_THETA ** (-jnp.arange(n_freq, dtype=F32) / n_freq)
    lane = np.arange(width) % dim
    is_col = (lane // (dim // 2)) == 1
    second = ((lane % (dim // 2)) // n_freq) == 1
    freq = inv_freq[lane % n_freq]
    ang_r = jnp.arange(s // GRID_W, dtype=F32)[:, None] * freq[None, :]
    ang_c = jnp.arange(GRID_W, dtype=F32)[:, None] * freq[None, :]

    def on_grid(f):
        return jnp.where(is_col[None, None, :], f(ang_c)[None, :, :], f(ang_r)[:, None, :]).reshape(s, width)

    cos, sin = on_grid(jnp.cos), on_grid(jnp.sin)
    sin_a = jnp.where(second[None, :], sin, 0.0)
    sin_b = jnp.where(second[None, :], 0.0, -sin)
    ident = jnp.ones((tc, width), F32)
    zero = jnp.zeros((tc, width), F32)
    return (jnp.concatenate([cos, ident], axis=0), jnp.concatenate([sin_a, zero], axis=0),
            jnp.concatenate([sin_b, zero], axis=0))


def kernel(x, c, ctx, c_ctx, w_mod, b_mod, norm1, w_in, diff_lambda, diff_norm, hgrn_lb_logits,
           hgrn_norm, q_norm, k_norm, w_out, norm2, w_ffn_in, w_ffn_out, final_norm):
    b, s, d = x.shape
    tc = ctx.shape[1]
    depth = w_mod.shape[0]
    assert tc == TILE and s % TILE == 0 and s % GRID_W == 0 and d == 4 * GROUP

    rope_a = _rope_tables(tc, s, A_QK, GROUP)
    rope_c = _rope_tables(tc, s, HEAD_DIM, GROUP)
    dft = _dft_tables(tc, s)
    hgrn_tabs = _hgrn_tables()
    hgrn_tabs = tuple(jnp.asarray(a, BF16 if i < 2 else F32) for i, a in enumerate(hgrn_tabs))
    bd = jnp.asarray(np.kron(np.eye(GROUP // HEAD_DIM), np.full((HEAD_DIM, HEAD_DIM), 1.0 / HEAD_DIM)), BF16)

    p = jax.nn.softmax(hgrn_lb_logits.astype(F32), axis=1)
    lower = jnp.cumsum(p, axis=1) - p[:, :1]

    def swap_heads(w, axis, start):
        cuts = [0, start + HEAD_DIM, start + 2 * HEAD_DIM, start + 3 * HEAD_DIM, w.shape[axis]]
        parts = [lax.slice_in_dim(w, cuts[i], cuts[i + 1], axis=axis) for i in range(4)]
        return jnp.concatenate([parts[0], parts[2], parts[1], parts[3]], axis=axis).astype(BF16)

    w_in_b = swap_heads(w_in, 2, COL_C_Q)
    w_out_b = swap_heads(w_out, 1, 2 * GROUP)
    w_ffn_in_b = w_ffn_in.astype(BF16)
    w_ffn_out_b = w_ffn_out.astype(BF16)

    rows = -(-(b + 1) // 8) * 8
    cc = jnp.zeros((rows, d), F32).at[:b].set(c).at[b].set(c_ctx)
    mod = _modulation(cc, w_mod, b_mod).reshape(depth, rows, 6, d)
    modsel = jnp.stack([jnp.broadcast_to(mod[:, b:b + 1], (depth, b, 6, d)), mod[:, :b]], axis=2)

    xs = (x, ctx)
    for l in range(depth):
        last = l == depth - 1
        lam_init = 0.8 - 0.6 * math.exp(-0.3 * l)
        proj = _in_proj(xs, s + tc, modsel[l], norm1[l], w_in_b, l)
        m_a, m_c = _attention(proj, rope_a, rope_c, diff_lambda[l], diff_norm[l], q_norm[l], k_norm[l],
                              bd, tc, lam_init)
        o_f, o_b = _hgrn(proj, lower[0, l], lower[1, l], hgrn_tabs)
        m_d = _fourier(proj, *dft, tc)
        if last and isinstance(xs, tuple):
            xs = xs[0]
        xs = _out_ffn(xs, m_a, o_f, o_b, proj, m_c, m_d, modsel[l], hgrn_norm[l], norm2[l], final_norm,
                      bd, w_out_b, w_ffn_in_b, w_ffn_out_b, l, s if last else s + tc, last)
    return xs
```

```python
import functools
import math

import numpy as np
import jax
import jax.numpy as jnp
from jax import lax
from jax.experimental import pallas as pl
from jax.experimental.pallas import tpu as pltpu

F32 = jnp.float32
BF16 = jnp.bfloat16

HEAD_DIM = 64
GRID_W = 64
ROPE_THETA = 10000.0
RMS_EPS = 1e-6
GROUP = 256
A_QK = 32
TILE = 256
CHUNK = 64
LEVELS = 6
LOG2E = 1.4426950408889634
VMEM_LIMIT = 56 * 1024 * 1024

COL_A_Q, COL_A_K, COL_A_V = 0, 256, 512
COL_B_Q, COL_B_FF, COL_B_FB, COL_B_I, COL_B_G = 768, 1024, 1280, 1536, 1792
COL_C_Q, COL_C_K, COL_C_V = 2048, 2304, 2432
COL_D_U = 2560


def _cparams(n_axes):
    return pltpu.CompilerParams(dimension_semantics=("arbitrary",) * n_axes,
                                vmem_limit_bytes=VMEM_LIMIT)


def _dot(a, b):
    return jnp.dot(a, b, preferred_element_type=F32)


def _dot_nt(a, b):
    return lax.dot_general(a, b, (((1,), (1,)), ((), ())), preferred_element_type=F32)


def _dot_tn(a, b):
    return lax.dot_general(a, b, (((0,), (0,)), ((), ())), preferred_element_type=F32)


def _group_mean_sq(y, bd):
    y2 = y * y
    hi = y2.astype(BF16)
    lo = (y2 - hi.astype(F32)).astype(BF16)
    return _dot(hi, bd) + _dot(lo, bd)


def _rope(x, cos, sin_a, sin_b, shift):
    w = x.shape[-1]
    return x * cos + pltpu.roll(x, shift, 1) * sin_a + pltpu.roll(x, w - shift, 1) * sin_b


def _mod_kernel(c_ref, w_ref, b_ref, o_ref):
    c = c_ref[...]
    a = c * jax.nn.sigmoid(c)
    o_ref[0] = _dot(a.astype(BF16), w_ref[0].astype(BF16)) + b_ref[0]


def _modulation(cc, w_mod, b_mod):
    depth, d, n = w_mod.shape
    r = cc.shape[0]
    tn = 1536
    return pl.pallas_call(
        _mod_kernel,
        grid=(depth, n // tn),
        in_specs=[pl.BlockSpec((r, d), lambda l, j: (0, 0)),
                  pl.BlockSpec((1, d, tn), lambda l, j: (l, 0, j)),
                  pl.BlockSpec((1, 1, tn), lambda l, j: (l, 0, j))],
        out_specs=pl.BlockSpec((1, r, tn), lambda l, j: (l, 0, j)),
        out_shape=jax.ShapeDtypeStruct((depth, r, n), F32),
        compiler_params=_cparams(2),
        name="modulation",
    )(cc, w_mod, b_mod.reshape(depth, 1, n))


def _mod_rows(mod_ref, sub, n_sub, with_ctx):
    m = mod_ref[0]
    if not (with_ctx and sub == n_sub - 1):
        return m[1]
    return jnp.where(pl.program_id(1) == pl.num_programs(1) - 1, m[0], m[1])


def _stream_rows(x_ref, ctx_ref, sub, n_sub):
    rows = slice(sub * TILE, (sub + 1) * TILE)
    if ctx_ref is None or sub < n_sub - 1:
        return x_ref[0, rows, :]
    return lax.cond(pl.program_id(1) == pl.num_programs(1) - 1,
                    lambda: ctx_ref[0], lambda: x_ref[0, rows, :])


def _inproj_kernel(*refs, split_input):
    x_ref, ctx_ref = (refs[0], refs[1]) if split_input else (refs[0], None)
    mod_ref, n1_ref, w_ref, o_ref = refs[-4:]
    n_sub = x_ref.shape[1] // TILE
    hs = []
    for i in range(n_sub):
        x = _stream_rows(x_ref, ctx_ref, i, n_sub)
        ms = jnp.mean(x * x, axis=-1, keepdims=True)
        y = x * lax.rsqrt(ms + RMS_EPS) * n1_ref[...]
        m = _mod_rows(mod_ref, i, n_sub, True)
        hs.append((y * (1.0 + m[1:2]) + m[0:1]).astype(BF16))
    o_ref[0] = _dot(jnp.concatenate(hs, axis=0), w_ref[...])


def _row_tile(t):
    return next(tm for tm in (3 * TILE, 2 * TILE, TILE) if t % tm == 0)


def _resident(shape):
    return pl.BlockSpec(shape, lambda i, j: (0,) * len(shape), pipeline_mode=pl.Buffered(1))


def _layer_resident(w, layer):
    return pl.BlockSpec((None,) + w.shape[1:], lambda i, j: (layer, 0, 0), pipeline_mode=pl.Buffered(1))


def _stream_specs(stream, t, tm):
    if not isinstance(stream, tuple):
        return (stream,), [pl.BlockSpec((1, tm, stream.shape[2]), lambda i, j: (i, j, 0))]
    x, ctx = stream
    assert x.shape[1] + ctx.shape[1] == t and ctx.shape[1] == TILE and t % tm == 0
    return stream, [pl.BlockSpec((1, tm, x.shape[2]), lambda i, j: (i, j, 0)),
                    pl.BlockSpec((1, TILE, x.shape[2]), lambda i, j: (i, 0, 0))]


def _in_proj(stream, t, modsel, norm1, w_in, layer):
    b, d = modsel.shape[0], modsel.shape[3]
    n = w_in.shape[2]
    tm = _row_tile(t)
    operands, specs = _stream_specs(stream, t, tm)
    return pl.pallas_call(
        functools.partial(_inproj_kernel, split_input=isinstance(stream, tuple)),
        grid=(b, t // tm),
        in_specs=specs + [pl.BlockSpec((1, 2, 6, d), lambda i, j: (i, 0, 0, 0)),
                          _resident((1, d)), _layer_resident(w_in, layer)],
        out_specs=pl.BlockSpec((1, tm, n), lambda i, j: (i, j, 0)),
        out_shape=jax.ShapeDtypeStruct((b, t, n), F32),
        compiler_params=_cparams(2),
        name="in_proj",
    )(*operands, modsel, norm1.reshape(1, d), w_in)


V_ROWS = HEAD_DIM + 16


KEY_TILES = (768, 256)


def _attend_t(groups, s_ref, lo, nk):
    key_tile = next(c for c in KEY_TILES if nk % c == 0)
    n_kt = nk // key_tile
    n_g = len(groups)
    maxes = [None] * n_g
    accs = [None] * n_g

    def keys(kt):
        return slice(lo + kt * key_tile, lo + (kt + 1) * key_tile)

    def score_tile(g, kt):
        kr_ref, wq, _, _ = groups[g]
        s = _dot_nt(kr_ref[keys(kt), :], wq)
        s_ref[g % 2, kt * key_tile:(kt + 1) * key_tile, :] = s
        m = jnp.max(s, axis=0, keepdims=True)
        maxes[g] = m if kt == 0 else jnp.maximum(maxes[g], m)

    def value_tile(g, kt, p):
        _, _, vt_ref, head = groups[g]
        pv = _dot(vt_ref[head, :, keys(kt)], p)
        accs[g] = pv if kt == 0 else accs[g] + pv

    for kt in range(n_kt):
        score_tile(0, kt)
    for g in range(n_g):
        p_prev = None
        for kt in range(n_kt):
            if g + 1 < n_g:
                score_tile(g + 1, kt)
            p = jnp.exp2(s_ref[g % 2, kt * key_tile:(kt + 1) * key_tile, :] - maxes[g]).astype(BF16)
            if kt > 0:
                value_tile(g, kt - 1, p_prev)
            p_prev = p
        value_tile(g, n_kt - 1, p_prev)
    return [a[0:HEAD_DIM] * (1.0 / a[HEAD_DIM:HEAD_DIM + 1]) for a in accs]


def _store_values_t(vt_ref, v):
    v_t = v.T
    for h in range(vt_ref.shape[0]):
        vt_ref[h, 0:HEAD_DIM, :] = v_t[h * HEAD_DIM:(h + 1) * HEAD_DIM, :].astype(BF16)
        vt_ref[h, HEAD_DIM:V_ROWS, :] = jnp.ones((V_ROWS - HEAD_DIM, v.shape[0]), BF16)


def _attn_kernel(aq_ref, ak_ref, av_ref, cq_ref, ck_ref, cv_ref,
                 a_cos_q, a_sa_q, a_sb_q, a_cos_k, a_sa_k, a_sb_k,
                 c_cos_q, c_sa_q, c_sb_q, c_cos_k, c_sa_k, c_sb_k,
                 lp_ref, gain_ref, qg_ref, kg_ref, bd_ref, oa_ref, oc_ref,
                 akr_ref, avt_ref, ckr_ref, cvt_ref, s_ref, *, tc, lam_init):
    t = pl.program_id(1)
    n_tok = akr_ref.shape[0]
    half = GROUP // 2
    bd = bd_ref[...]

    @pl.when(t == 0)
    def _():
        kr = _rope(ak_ref[0], a_cos_k[...], a_sa_k[...], a_sb_k[...], A_QK // 4)
        akr_ref[...] = kr.astype(BF16)
        _store_values_t(avt_ref, av_ref[0])
        k = ck_ref[0]
        kn = k * lax.rsqrt(_group_mean_sq(k, bd[0:half, 0:half]) + RMS_EPS) * kg_ref[...]
        ckr_ref[...] = _rope(kn, c_cos_k[...], c_sa_k[...], c_sb_k[...], HEAD_DIM // 4).astype(BF16)
        _store_values_t(cvt_ref, cv_ref[0])

    lp = lp_ref[...]
    lam = (jnp.exp(jnp.sum(lp[0:1] * lp[1:2], axis=-1, keepdims=True))
           - jnp.exp(jnp.sum(lp[2:3] * lp[3:4], axis=-1, keepdims=True)) + lam_init)
    lane = lax.broadcasted_iota(jnp.int32, (1, GROUP), 1)
    lane_h = lax.broadcasted_iota(jnp.int32, (1, half), 1)
    n_heads = GROUP // HEAD_DIM

    def attend(lo, nk, n_sub):
        groups = []
        for u in range(n_sub):
            r = slice(u * TILE, (u + 1) * TILE)
            qa = _rope(aq_ref[0, r, :], a_cos_q[r, :], a_sa_q[r, :], a_sb_q[r, :], A_QK // 4)
            qa = qa * (A_QK ** -0.5 * LOG2E)
            q = cq_ref[0, r, :]
            qn = q * lax.rsqrt(_group_mean_sq(q, bd) + RMS_EPS) * qg_ref[...]
            qc = _rope(qn, c_cos_q[r, :], c_sa_q[r, :], c_sb_q[r, :], HEAD_DIM // 4)
            qc = qc * (HEAD_DIM ** -0.5 * LOG2E)
            groups += [(akr_ref, jnp.concatenate([jnp.where(lane // A_QK == 2 * h + c, qa, 0.0).astype(BF16)
                                                  for c in range(2)], axis=0), avt_ref, h)
                       for h in range(n_heads)]
            groups += [(ckr_ref, jnp.concatenate(
                [jnp.where(lane_h // HEAD_DIM == g, qc[:, qb * half:(qb + 1) * half], 0.0).astype(BF16)
                 for qb in range(2)], axis=0), cvt_ref, g) for g in range(2)]
        outs = _attend_t(groups, s_ref, lo, nk)
        per_sub = n_heads + 2
        for u in range(n_sub):
            r = slice(u * TILE, (u + 1) * TILE)
            o = outs[u * per_sub:(u + 1) * per_sub]
            heads = [x[:, 0:TILE] - lam * x[:, TILE:2 * TILE] for x in o[0:n_heads]]
            acc = jnp.concatenate(heads, axis=0).T
            y = acc * lax.rsqrt(_group_mean_sq(acc, bd) + RMS_EPS)
            oa_ref[0, r, :] = y * gain_ref[...] * (1.0 - lam_init)
            heads = [o[n_heads + g][:, qb * TILE:(qb + 1) * TILE] for qb in range(2) for g in range(2)]
            oc_ref[0, r, :] = jnp.concatenate(heads, axis=0).T

    @pl.when(t == pl.num_programs(1) - 1)
    def _():
        attend(n_tok - tc, tc, 1)

    @pl.when(t < pl.num_programs(1) - 1)
    def _():
        attend(0, n_tok, aq_ref.shape[1] // TILE)


def _attention(proj, rope_a, rope_c, lam_params, sub_gain, q_gain, k_gain, bd, tc, lam_init):
    b, t, _ = proj.shape
    half = GROUP // 2
    gain = jnp.tile(sub_gain, GROUP // HEAD_DIM).reshape(1, GROUP)
    qg = jnp.tile(q_gain, GROUP // HEAD_DIM).reshape(1, GROUP)
    kg = jnp.tile(k_gain, half // HEAD_DIM).reshape(1, half)

    tq = 2 * TILE
    assert (t - tc) % tq == 0 and tc == TILE

    def rows(width, col):
        return pl.BlockSpec((1, tq, width), lambda i, j: (i, j, col // width))

    def sample(width, col):
        return pl.BlockSpec((1, t, width), lambda i, j: (i, 0, col // width))

    tile_spec = pl.BlockSpec((tq, GROUP), lambda i, j: (j, 0))
    out_spec = pl.BlockSpec((1, tq, GROUP), lambda i, j: (i, j, 0))
    return pl.pallas_call(
        functools.partial(_attn_kernel, tc=tc, lam_init=lam_init),
        grid=(b, (t - tc) // tq + 1),
        in_specs=[rows(GROUP, COL_A_Q), sample(GROUP, COL_A_K), sample(GROUP, COL_A_V),
                  rows(GROUP, COL_C_Q), sample(half, COL_C_K), sample(half, COL_C_V),
                  tile_spec, tile_spec, tile_spec,
                  _resident((t, GROUP)), _resident((t, GROUP)), _resident((t, GROUP)),
                  tile_spec, tile_spec, tile_spec,
                  _resident((t, half)), _resident((t, half)), _resident((t, half)),
                  _resident(lam_params.shape), _resident((1, GROUP)), _resident((1, GROUP)),
                  _resident((1, half)), _resident((GROUP, GROUP))],
        out_specs=[out_spec, out_spec],
        out_shape=[jax.ShapeDtypeStruct((b, t, GROUP), F32)] * 2,
        scratch_shapes=[pltpu.VMEM((t, GROUP), BF16),
                        pltpu.VMEM((GROUP // HEAD_DIM, V_ROWS, t), BF16),
                        pltpu.VMEM((t, half), BF16),
                        pltpu.VMEM((half // HEAD_DIM, V_ROWS, t), BF16),
                        pltpu.VMEM((2, t, 2 * TILE), F32)],
        compiler_params=_cparams(2),
        name="attention",
    )(proj, proj, proj, proj, proj, proj, *rope_a, *rope_a, *rope_c, *rope_c,
      lam_params, gain, qg, kg, bd)


def _fourier_stage_kernel(ux_ref, uc_ref, c64_ref, dctx_ref, w_ref, o_ref):
    s = ux_ref.shape[1]
    tc = uc_ref.shape[1]
    w = _dot(ux_ref[0].astype(BF16), c64_ref[...])
    w_ref[0:s, :] = w[:, 0:GROUP].astype(BF16)
    w_ref[s:2 * s, :] = w[:, GROUP:2 * GROUP].astype(BF16)
    wc = _dot(uc_ref[0].astype(BF16), c64_ref[...])
    wc = jnp.concatenate([wc[:, 0:GROUP], wc[:, GROUP:2 * GROUP]], axis=0).astype(BF16)
    o_ref[0] = _dot(dctx_ref[...], wc) * (tc ** -0.5)


def _fourier_dft_kernel(d_ref, w_ref, alias_ref, o_ref):
    del alias_ref
    o_ref[0] = _dot(d_ref[...], w_ref[...]) * ((d_ref.shape[1] // 2) ** -0.5)


def _fourier(proj, c64, dctx, dx, tc):
    b, t, _ = proj.shape
    s = t - tc
    col = COL_D_U // GROUP
    w_all, o_ctx = pl.pallas_call(
        _fourier_stage_kernel,
        grid=(b,),
        in_specs=[pl.BlockSpec((1, s, GROUP), lambda i: (i, 0, col)),
                  pl.BlockSpec((1, tc, GROUP), lambda i: (i, s // tc, col)),
                  pl.BlockSpec((GROUP, 2 * GROUP), lambda i: (0, 0)),
                  pl.BlockSpec((tc, 2 * tc), lambda i: (0, 0))],
        out_specs=[pl.BlockSpec((2 * s, GROUP), lambda i: (0, i)),
                   pl.BlockSpec((1, tc, GROUP), lambda i: (i, s // tc, 0))],
        out_shape=[jax.ShapeDtypeStruct((2 * s, b * GROUP), BF16),
                   jax.ShapeDtypeStruct((b, t, GROUP), F32)],
        compiler_params=_cparams(1),
        name="fourier_stage",
    )(proj, proj, c64, dctx)
    tm = next(c for c in (4 * TILE, 2 * TILE) if s % c == 0)
    return pl.pallas_call(
        _fourier_dft_kernel,
        grid=(s // tm, b),
        in_specs=[pl.BlockSpec((tm, 2 * s), lambda i, j: (i, 0)),
                  pl.BlockSpec((2 * s, GROUP), lambda i, j: (0, j)),
                  pl.BlockSpec(memory_space=pl.ANY)],
        out_specs=pl.BlockSpec((1, tm, GROUP), lambda i, j: (j, i, 0)),
        out_shape=jax.ShapeDtypeStruct((b, t, GROUP), F32),
        input_output_aliases={2: 0},
        compiler_params=_cparams(2),
        name="fourier_dft",
    )(dx, w_all, o_ctx)


def _dft_tables(tc, s):
    def cs_direct(rows, n, period, quarter_turn=False):
        j = jnp.arange(rows, dtype=jnp.int32)
        k = jnp.arange(n, dtype=jnp.int32)
        ang = ((j[:, None] * k[None, :]) % period).astype(F32) * (2.0 * math.pi / period)
        if quarter_turn:
            ang = jnp.concatenate([ang, ang + 0.5 * math.pi], axis=1)
        return jnp.cos(ang), jnp.sin(ang)

    def cos_neg_sin(n):
        ca, sa = cs_direct(n // GRID_W, n, n // GRID_W)
        ca, sa = jnp.tile(ca, (1, 2)), jnp.tile(sa, (1, 2))
        cb, sb = cs_direct(GRID_W, n, n, quarter_turn=True)
        cos = ca[:, None, :] * cb[None, :, :] - sa[:, None, :] * sb[None, :, :]
        return cos.reshape(n, 2 * n).astype(BF16)

    c64, s64 = cs_direct(HEAD_DIM, HEAD_DIM, HEAD_DIM)
    eye = jnp.eye(GROUP // HEAD_DIM, dtype=F32)
    c64 = jnp.concatenate([jnp.kron(eye, c64), jnp.kron(eye, s64)], axis=1) * (HEAD_DIM ** -0.5)
    return c64.astype(BF16), cos_neg_sin(tc), cos_neg_sin(s)


def _hgrn_tables():
    c = CHUNK
    idx = np.arange(c)
    tri = (idx[None, :] <= idx[:, None]).astype(np.float32)
    lvl = np.full((c, c), -1.0, np.float32)
    lvl[idx, idx] = 0.0
    rows = [tri]
    for l in range(1, LEVELS + 1):
        w = c >> (l - 1)
        blk, pos = idx // w, idx % w
        ref = blk * w + w // 2 - 1
        rows.append(tri[ref])
        m = (blk[:, None] == blk[None, :]) & (pos[:, None] >= w // 2) & (pos[None, :] < w // 2)
        lvl[m] = float(l)
    w_f = np.concatenate(rows, axis=0)
    w_b = w_f.reshape(LEVELS + 1, c, c)[:, ::-1, ::-1].reshape(-1, c)
    heads = GROUP // HEAD_DIM
    head_mask = np.kron(np.eye(heads, dtype=np.float32), np.ones((c, HEAD_DIM), np.float32))
    return (np.tile(w_f, (1, 3)), np.tile(w_b, (1, 3)),
            np.tile(lvl, (1, heads)), np.tile(lvl.T, (1, heads)), head_mask)


HGRN_FAST_RANGE = 75.0


def _hgrn_gates(z, lb):
    e = jnp.exp(-jnp.abs(z))
    r = 1.0 / (1.0 + e)
    sig_pos = jnp.where(z >= 0, r, e * r)
    sig_neg = jnp.where(z >= 0, e * r, r)
    return jnp.log(lb + (1.0 - lb) * sig_pos), (1.0 - lb) * sig_neg


def _split3(g):
    g_hi = g.astype(BF16)
    r1 = g - g_hi.astype(F32)
    g_mid = r1.astype(BF16)
    g_lo = (r1 - g_mid.astype(F32)).astype(BF16)
    return jnp.concatenate([g_hi, g_mid, g_lo], axis=0)


def _block_diag(x, hm_b):
    return jnp.concatenate([x.astype(BF16)] * (GROUP // HEAD_DIM), axis=0) * hm_b


def _hgrn_output(ch, a, hm, hm_b, st_ref, end_row):
    q, k, v, b = ch["q"], ch["k"], ch["v"], ch["b"]
    b_end = b[end_row:end_row + 1]
    st = st_ref[...]
    o = _dot(a.astype(BF16), _block_diag(v, hm_b))
    o = o + _dot_nt((q * jnp.exp(b)).astype(BF16), st.astype(BF16))
    upd = _dot_tn(v.astype(BF16), (k * jnp.exp(b_end - b)).astype(BF16))
    st_ref[...] = st * jnp.exp(b_end) + upd * hm
    return o


def _hgrn_scores_fast(ch, lvl, hm_b):
    d = ch["b"] - ch["b_mid"]
    a = _dot_nt((ch["q"] * jnp.exp(d)).astype(BF16), _block_diag(ch["k"] * jnp.exp(-d), hm_b))
    return jnp.where(lvl >= 0.0, a, 0.0)


def _hgrn_scores_levels(ch, w3, lvl, hm_b):
    c = CHUNK
    q, k = ch["q"], ch["k"]
    sums = _dot(w3, ch["g3"])
    b = sums[0:c]
    a = jnp.where(lvl == 0.0, _dot_nt(q.astype(BF16), _block_diag(k, hm_b)), 0.0)
    for l in range(1, LEVELS + 1):
        d = jnp.exp(-jnp.abs(b - sums[l * c:(l + 1) * c]))
        a = jnp.where(lvl == float(l), _dot_nt((q * d).astype(BF16), _block_diag(k * d, hm_b)), a)
    return a


def _hgrn_kernel(qf_ref, zf_ref, vf_ref, qb_ref, zb_ref, vb_ref, lbf_ref, lbb_ref,
                 w3f_ref, w3b_ref, lvf_ref, lvb_ref, hm_ref, of_ref, ob_ref, sf_ref, sb_ref,
                 sf_in_ref, sb_in_ref):
    @pl.when(pl.program_id(1) == 0)
    def _():
        sf_ref[...] = jnp.zeros_like(sf_ref)
        sb_ref[...] = jnp.zeros_like(sb_ref)

    hm = hm_ref[...]
    hm_b = hm.astype(BF16)
    n = TILE // CHUNK
    fwd = dict(q=qf_ref, z=zf_ref, v=vf_ref, lb=lbf_ref, w3=w3f_ref, lvl=lvf_ref, o=of_ref, st=sf_ref,
               st_in=sf_in_ref, end=CHUNK - 1)
    bwd = dict(q=qb_ref, z=zb_ref, v=vb_ref, lb=lbb_ref, w3=w3b_ref, lvl=lvb_ref, o=ob_ref, st=sb_ref,
               st_in=sb_in_ref, end=0)
    order = [(d, ci if d is fwd else n - 1 - ci) for ci in range(n) for d in (fwd, bwd)]

    def chunk(d, ci, w3_rows):
        rows = slice(ci * CHUNK, (ci + 1) * CHUNK)
        g, k = _hgrn_gates(d["z"][0, rows, :], d["lb"][...])
        g3 = _split3(g)
        sums = _dot(d["w3"][0:w3_rows, :], g3)
        return dict(q=d["q"][0, rows, :], v=d["v"][0, rows, :], k=k, g3=g3,
                    b=sums[0:CHUNK], b_mid=sums[CHUNK:2 * CHUNK])

    for d in (fwd, bwd):
        d["st_in"][...] = d["st"][...]
    chunks = [chunk(d, ci, 2 * CHUNK) for d, ci in order]
    spread = None
    for (d, ci), ch in zip(order, chunks):
        dist = jnp.abs(ch["b"] - ch["b_mid"])
        spread = dist if spread is None else jnp.maximum(spread, dist)
        a = _hgrn_scores_fast(ch, d["lvl"][...], hm_b)
        d["o"][0, ci * CHUNK:(ci + 1) * CHUNK, :] = _hgrn_output(ch, a, hm, hm_b, d["st"], d["end"])

    @pl.when(jnp.logical_not(jnp.max(spread) < HGRN_FAST_RANGE))
    def _():
        for d in (fwd, bwd):
            d["st"][...] = d["st_in"][...]
        for d, ci in order:
            ch = chunk(d, ci, 2 * CHUNK)
            a = _hgrn_scores_levels(ch, d["w3"][...], d["lvl"][...], hm_b)
            d["o"][0, ci * CHUNK:(ci + 1) * CHUNK, :] = _hgrn_output(ch, a, hm, hm_b, d["st"], d["end"])


def _hgrn(proj, lb_f, lb_b, tables):
    b, t, _ = proj.shape
    nt = t // TILE
    w3f, w3b, lvf, lvb, hm = tables

    def fwd(col):
        return pl.BlockSpec((1, TILE, GROUP),
                            lambda i, j: (i, jnp.where(j == 0, nt - 1, j - 1), col // GROUP))

    def bwd(col):
        return pl.BlockSpec((1, TILE, GROUP), lambda i, j: (i, nt - 1 - j, col // GROUP))

    def const(a):
        return pl.BlockSpec(a.shape, lambda i, j: (0,) * a.ndim)

    out_f = fwd(0)
    out_b = bwd(0)
    lb_f = lb_f.reshape(1, GROUP)
    lb_b = lb_b.reshape(1, GROUP)
    return pl.pallas_call(
        _hgrn_kernel,
        grid=(b, nt),
        in_specs=[fwd(COL_B_Q), fwd(COL_B_FF), fwd(COL_B_I), bwd(COL_B_Q), bwd(COL_B_FB), bwd(COL_B_I),
                  const(lb_f), const(lb_b), const(w3f), const(w3b), const(lvf), const(lvb), const(hm)],
        out_specs=[out_f, out_b],
        out_shape=[jax.ShapeDtypeStruct((b, t, GROUP), F32)] * 2,
        scratch_shapes=[pltpu.VMEM((GROUP, GROUP), F32)] * 4,
        compiler_params=_cparams(2),
        name="hgrn2",
    )(proj, proj, proj, proj, proj, proj, lb_f, lb_b, w3f, w3b, lvf, lvb, hm)


def _outffn_kernel(*refs, hidden, final, split_input):
    x_ref, ctx_ref = (refs[0], refs[1]) if split_input else (refs[0], None)
    (ma_ref, of_ref, ob_ref, g_ref, mc_ref, md_ref, mod_ref, hg_ref, n2_ref, fn_ref, bd_ref,
     wo_ref, wi_ref, wd_ref, o_ref, act_ref) = refs[-16:]
    n_sub = x_ref.shape[1] // TILE
    o = of_ref[0] + ob_ref[0]
    g = g_ref[0]
    mb = (o * lax.rsqrt(_group_mean_sq(o, bd_ref[...]) + RMS_EPS) * hg_ref[...]) * (g * jax.nn.sigmoid(g))
    mix = jnp.concatenate([ma_ref[0], mb, mc_ref[0], md_ref[0]], axis=-1).astype(BF16)
    y = _dot(mix, wo_ref[...])
    xs, hs, gates = [], [], []
    for i in range(n_sub):
        m = _mod_rows(mod_ref, i, n_sub, not final)
        x = _stream_rows(x_ref, ctx_ref, i, n_sub) + m[2:3] * y[i * TILE:(i + 1) * TILE]
        ms = jnp.mean(x * x, axis=-1, keepdims=True)
        xs.append(x)
        hs.append((x * lax.rsqrt(ms + RMS_EPS) * (n2_ref[...] * (1.0 + m[4:5])) + m[3:4]).astype(BF16))
        gates.append(m[5:6])
    h = jnp.concatenate(hs, axis=0)
    step = 256
    for c in range(hidden // step):
        gate = _dot(h, wi_ref[:, c * step:(c + 1) * step])
        up = _dot(h, wi_ref[:, hidden + c * step:hidden + (c + 1) * step])
        act_ref[:, c * step:(c + 1) * step] = (gate * jax.nn.sigmoid(gate) * up).astype(BF16)
    acc = _dot(act_ref[...], wd_ref[...])
    for i in range(n_sub):
        x = xs[i] + gates[i] * acc[i * TILE:(i + 1) * TILE]
        if final:
            ms = jnp.mean(x * x, axis=-1, keepdims=True)
            x = x * lax.rsqrt(ms + RMS_EPS) * fn_ref[...]
        o_ref[0, i * TILE:(i + 1) * TILE, :] = x


def _out_ffn(xs, ma, o_f, o_b, proj, mc, md, modsel, hgrn_gain, norm2, final_norm, bd, w_out,
             w_ffn_in, w_ffn_out, layer, n_rows, final):
    b, d = modsel.shape[0], modsel.shape[3]
    hidden = w_ffn_out.shape[1]
    tm = _row_tile(n_rows) if not final else 2 * TILE

    def rows(width, col=0):
        return pl.BlockSpec((1, tm, width), lambda i, j: (i, j, col))

    hg = jnp.tile(hgrn_gain, GROUP // HEAD_DIM).reshape(1, GROUP)
    operands, specs = _stream_specs(xs, n_rows, tm)
    return pl.pallas_call(
        functools.partial(_outffn_kernel, hidden=hidden, final=final, split_input=isinstance(xs, tuple)),
        grid=(b, n_rows // tm),
        in_specs=specs + [rows(GROUP), rows(GROUP), rows(GROUP), rows(GROUP, COL_B_G // GROUP),
                          rows(GROUP), rows(GROUP),
                          pl.BlockSpec((1, 2, 6, d), lambda i, j: (i, 0, 0, 0)),
                          _resident((1, GROUP)), _resident((1, d)), _resident((1, d)),
                          _resident((GROUP, GROUP)), _layer_resident(w_out, layer),
                          _layer_resident(w_ffn_in, layer), _layer_resident(w_ffn_out, layer)],
        out_specs=rows(d),
        out_shape=jax.ShapeDtypeStruct((b, n_rows, d), F32),
        scratch_shapes=[pltpu.VMEM((tm, hidden), BF16)],
        compiler_params=_cparams(2),
        name="out_ffn",
    )(*operands, ma, o_f, o_b, proj, mc, md, modsel, hg, norm2.reshape(1, d), final_norm.reshape(1, d),
      bd, w_out, w_ffn_in, w_ffn_out)


def _rope_tables(tc, s, dim, width):
    n_freq = dim // 4
    inv_freq = ROPE_THETA ** (-jnp.arange(n_freq, dtype=F32) / n_freq)
    lane = np.arange(width) % dim
    is_col = (lane // (dim // 2)) == 1
    second = ((lane % (dim // 2)) // n_freq) == 1
    freq = inv_freq[lane % n_freq]
    ang_r = jnp.arange(s // GRID_W, dtype=F32)[:, None] * freq[None, :]
    ang_c = jnp.arange(GRID_W, dtype=F32)[:, None] * freq[None, :]

    def on_grid(f):
        return jnp.where(is_col[None, None, :], f(ang_c)[None, :, :], f(ang_r)[:, None, :]).reshape(s, width)

    cos, sin = on_grid(jnp.cos), on_grid(jnp.sin)
    sin_a = jnp.where(second[None, :], sin, 0.0)
    sin_b = jnp.where(second[None, :], 0.0, -sin)
    ident = jnp.ones((tc, width), F32)
    zero = jnp.zeros((tc, width), F32)
    return (jnp.concatenate([cos, ident], axis=0), jnp.concatenate([sin_a, zero], axis=0),
            jnp.concatenate([sin_b, zero], axis=0))


def kernel(x, c, ctx, c_ctx, w_mod, b_mod, norm1, w_in, diff_lambda, diff_norm, hgrn_lb_logits,
           hgrn_norm, q_norm, k_norm, w_out, norm2, w_ffn_in, w_ffn_out, final_norm):
    b, s, d = x.shape
    tc = ctx.shape[1]
    depth = w_mod.shape[0]
    assert tc == TILE and s % TILE == 0 and s % GRID_W == 0 and d == 4 * GROUP

    rope_a = _rope_tables(tc, s, A_QK, GROUP)
    rope_c = _rope_tables(tc, s, HEAD_DIM, GROUP)
    dft = _dft_tables(tc, s)
    hgrn_tabs = _hgrn_tables()
    hgrn_tabs = tuple(jnp.asarray(a, BF16 if i < 2 else F32) for i, a in enumerate(hgrn_tabs))
    bd = jnp.asarray(np.kron(np.eye(GROUP // HEAD_DIM), np.full((HEAD_DIM, HEAD_DIM), 1.0 / HEAD_DIM)), BF16)

    p = jax.nn.softmax(hgrn_lb_logits.astype(F32), axis=1)
    lower = jnp.cumsum(p, axis=1) - p[:, :1]

    def swap_heads(w, axis, start):
        cuts = [0, start + HEAD_DIM, start + 2 * HEAD_DIM, start + 3 * HEAD_DIM, w.shape[axis]]
        parts = [lax.slice_in_dim(w, cuts[i], cuts[i + 1], axis=axis) for i in range(4)]
        return jnp.concatenate([parts[0], parts[2], parts[1], parts[3]], axis=axis).astype(BF16)

    w_in_b = swap_heads(w_in, 2, COL_C_Q)
    w_out_b = swap_heads(w_out, 1, 2 * GROUP)
    w_ffn_in_b = w_ffn_in.astype(BF16)
    w_ffn_out_b = w_ffn_out.astype(BF16)

    rows = -(-(b + 1) // 8) * 8
    cc = jnp.zeros((rows, d), F32).at[:b].set(c).at[b].set(c_ctx)
    mod = _modulation(cc, w_mod, b_mod).reshape(depth, rows, 6, d)
    modsel = jnp.stack([jnp.broadcast_to(mod[:, b:b + 1], (depth, b, 6, d)), mod[:, :b]], axis=2)

    xs = (x, ctx)
    for l in range(depth):
        last = l == depth - 1
        lam_init = 0.8 - 0.6 * math.exp(-0.3 * l)
        proj = _in_proj(xs, s + tc, modsel[l], norm1[l], w_in_b, l)
        m_a, m_c = _attention(proj, rope_a, rope_c, diff_lambda[l], diff_norm[l], q_norm[l], k_norm[l],
                              bd, tc, lam_init)
        o_f, o_b = _hgrn(proj, lower[0, l], lower[1, l], hgrn_tabs)
        m_d = _fourier(proj, *dft, tc)
        if last and isinstance(xs, tuple):
            xs = xs[0]
        xs = _out_ffn(xs, m_a, o_f, o_b, proj, m_c, m_d, modsel[l], hgrn_norm[l], norm2[l], final_norm,
                      bd, w_out_b, w_ffn_in_b, w_ffn_out_b, l, s if last else s + tc, last)
    return xs
```

```python
import functools
import math

import numpy as np
import jax
import jax.numpy as jnp
from jax import lax
from jax.experimental import pallas as pl
from jax.experimental.pallas import tpu as pltpu

F32 = jnp.float32
BF16 = jnp.bfloat16

HEAD_DIM = 64
GRID_W = 64
ROPE_THETA = 10000.0
RMS_EPS = 1e-6
GROUP = 256
A_QK = 32
TILE = 256
CHUNK = 64
LEVELS = 6
LOG2E = 1.4426950408889634
VMEM_LIMIT = 56 * 1024 * 1024

COL_A_Q, COL_A_K, COL_A_V = 0, 256, 512
COL_B_Q, COL_B_FF, COL_B_FB, COL_B_I, COL_B_G = 768, 1024, 1280, 1536, 1792
COL_C_Q, COL_C_K, COL_C_V = 2048, 2304, 2432
COL_D_U = 2560


def _cparams(n_axes):
    return pltpu.CompilerParams(dimension_semantics=("arbitrary",) * n_axes,
                                vmem_limit_bytes=VMEM_LIMIT)


def _dot(a, b):
    return jnp.dot(a, b, preferred_element_type=F32)


def _dot_nt(a, b):
    return lax.dot_general(a, b, (((1,), (1,)), ((), ())), preferred_element_type=F32)


def _dot_tn(a, b):
    return lax.dot_general(a, b, (((0,), (0,)), ((), ())), preferred_element_type=F32)


def _group_mean_sq(y, bd):
    y2 = y * y
    hi = y2.astype(BF16)
    lo = (y2 - hi.astype(F32)).astype(BF16)
    return _dot(hi, bd) + _dot(lo, bd)


def _rope(x, cos, sin_a, sin_b, shift):
    w = x.shape[-1]
    return x * cos + pltpu.roll(x, shift, 1) * sin_a + pltpu.roll(x, w - shift, 1) * sin_b


def _mod_kernel(c_ref, w_ref, b_ref, o_ref):
    c = c_ref[...]
    a = c * jax.nn.sigmoid(c)
    o_ref[0] = _dot(a.astype(BF16), w_ref[0].astype(BF16)) + b_ref[0]


def _modulation(cc, w_mod, b_mod):
    depth, d, n = w_mod.shape
    r = cc.shape[0]
    tn = 1536
    return pl.pallas_call(
        _mod_kernel,
        grid=(depth, n // tn),
        in_specs=[pl.BlockSpec((r, d), lambda l, j: (0, 0)),
                  pl.BlockSpec((1, d, tn), lambda l, j: (l, 0, j)),
                  pl.BlockSpec((1, 1, tn), lambda l, j: (l, 0, j))],
        out_specs=pl.BlockSpec((1, r, tn), lambda l, j: (l, 0, j)),
        out_shape=jax.ShapeDtypeStruct((depth, r, n), F32),
        compiler_params=_cparams(2),
        name="modulation",
    )(cc, w_mod, b_mod.reshape(depth, 1, n))


def _mod_rows(mod_ref, sub, n_sub, with_ctx):
    m = mod_ref[0]
    if not (with_ctx and sub == n_sub - 1):
        return m[1]
    return jnp.where(pl.program_id(1) == pl.num_programs(1) - 1, m[0], m[1])


def _stream_rows(x_ref, ctx_ref, sub, n_sub):
    rows = slice(sub * TILE, (sub + 1) * TILE)
    if ctx_ref is None or sub < n_sub - 1:
        return x_ref[0, rows, :]
    return lax.cond(pl.program_id(1) == pl.num_programs(1) - 1,
                    lambda: ctx_ref[0], lambda: x_ref[0, rows, :])


def _inproj_kernel(*refs, split_input):
    x_ref, ctx_ref = (refs[0], refs[1]) if split_input else (refs[0], None)
    mod_ref, n1_ref, w_ref, o_ref = refs[-4:]
    n_sub = x_ref.shape[1] // TILE
    for i in range(n_sub):
        x = _stream_rows(x_ref, ctx_ref, i, n_sub)
        ms = jnp.mean(x * x, axis=-1, keepdims=True)
        m = _mod_rows(mod_ref, i, n_sub, True)
        h = x * lax.rsqrt(ms + RMS_EPS) * (n1_ref[...] * (1.0 + m[1:2])) + m[0:1]
        o_ref[0, i * TILE:(i + 1) * TILE, :] = _dot(h.astype(BF16), w_ref[...])


def _row_tile(t):
    return next(tm for tm in (3 * TILE, 2 * TILE, TILE) if t % tm == 0)


def _resident(shape):
    return pl.BlockSpec(shape, lambda i, j: (0,) * len(shape), pipeline_mode=pl.Buffered(1))


def _layer_resident(w, layer):
    return pl.BlockSpec((None,) + w.shape[1:], lambda i, j: (layer, 0, 0), pipeline_mode=pl.Buffered(1))


def _stream_specs(stream, t, tm):
    if not isinstance(stream, tuple):
        return (stream,), [pl.BlockSpec((1, tm, stream.shape[2]), lambda i, j: (i, j, 0))]
    x, ctx = stream
    assert x.shape[1] + ctx.shape[1] == t and ctx.shape[1] == TILE and t % tm == 0
    last = (x.shape[1] - 1) // tm
    return stream, [pl.BlockSpec((1, tm, x.shape[2]), lambda i, j: (i, jnp.minimum(j, last), 0)),
                    pl.BlockSpec((1, TILE, x.shape[2]), lambda i, j: (i, 0, 0))]


def _in_proj(stream, t, modsel, norm1, w_in, layer):
    b, d = modsel.shape[0], modsel.shape[3]
    n = w_in.shape[2]
    tm = _row_tile(t)
    operands, specs = _stream_specs(stream, t, tm)
    return pl.pallas_call(
        functools.partial(_inproj_kernel, split_input=isinstance(stream, tuple)),
        grid=(b, t // tm),
        in_specs=specs + [pl.BlockSpec((1, 2, 6, d), lambda i, j: (i, 0, 0, 0)),
                          _resident((1, d)), _layer_resident(w_in, layer)],
        out_specs=pl.BlockSpec((1, tm, n), lambda i, j: (i, j, 0)),
        out_shape=jax.ShapeDtypeStruct((b, t, n), F32),
        compiler_params=_cparams(2),
        name="in_proj",
    )(*operands, modsel, norm1.reshape(1, d), w_in)


V_ROWS = HEAD_DIM + 16


KEY_TILES = (768, 256)


def _attend_t(groups, s_ref, lo, nk):
    key_tile = next(c for c in KEY_TILES if nk % c == 0)
    n_kt = nk // key_tile
    n_g = len(groups)
    maxes = [None] * n_g
    accs = [None] * n_g

    def keys(kt):
        return slice(lo + kt * key_tile, lo + (kt + 1) * key_tile)

    def score_tile(g, kt):
        kr_ref, wq, _, _ = groups[g]
        s = _dot_nt(kr_ref[keys(kt), :], wq)
        s_ref[g % 2, kt * key_tile:(kt + 1) * key_tile, :] = s
        m = jnp.max(s, axis=0, keepdims=True)
        maxes[g] = m if kt == 0 else jnp.maximum(maxes[g], m)

    def value_tile(g, kt, p):
        _, _, vt_ref, head = groups[g]
        pv = _dot(vt_ref[head, :, keys(kt)], p)
        accs[g] = pv if kt == 0 else accs[g] + pv

    for kt in range(n_kt):
        score_tile(0, kt)
    for g in range(n_g):
        p_prev = None
        for kt in range(n_kt):
            if g + 1 < n_g:
                score_tile(g + 1, kt)
            p = jnp.exp2(s_ref[g % 2, kt * key_tile:(kt + 1) * key_tile, :] - maxes[g]).astype(BF16)
            if kt > 0:
                value_tile(g, kt - 1, p_prev)
            p_prev = p
        value_tile(g, n_kt - 1, p_prev)
    return [a[0:HEAD_DIM] * (1.0 / a[HEAD_DIM:HEAD_DIM + 1]) for a in accs]


def _store_values_t(vt_ref, v):
    v_t = v.T
    for h in range(vt_ref.shape[0]):
        vt_ref[h, 0:HEAD_DIM, :] = v_t[h * HEAD_DIM:(h + 1) * HEAD_DIM, :].astype(BF16)
        vt_ref[h, HEAD_DIM:V_ROWS, :] = jnp.ones((V_ROWS - HEAD_DIM, v.shape[0]), BF16)


def _attn_kernel(aq_ref, ak_ref, av_ref, cq_ref, ck_ref, cv_ref,
                 a_cos_q, a_sa_q, a_sb_q, a_cos_k, a_sa_k, a_sb_k,
                 c_cos_q, c_sa_q, c_sb_q, c_cos_k, c_sa_k, c_sb_k,
                 lp_ref, gain_ref, qg_ref, kg_ref, bd_ref, oa_ref, oc_ref,
                 akr_ref, avt_ref, ckr_ref, cvt_ref, s_ref, *, tc, lam_init):
    t = pl.program_id(1)
    n_tok = akr_ref.shape[0]
    half = GROUP // 2
    bd = bd_ref[...]

    @pl.when(t == 0)
    def _():
        kr = _rope(ak_ref[0], a_cos_k[...], a_sa_k[...], a_sb_k[...], A_QK // 4)
        akr_ref[...] = kr.astype(BF16)
        _store_values_t(avt_ref, av_ref[0])
        k = ck_ref[0]
        kn = k * lax.rsqrt(_group_mean_sq(k, bd[0:half, 0:half]) + RMS_EPS) * kg_ref[...]
        ckr_ref[...] = _rope(kn, c_cos_k[...], c_sa_k[...], c_sb_k[...], HEAD_DIM // 4).astype(BF16)
        _store_values_t(cvt_ref, cv_ref[0])

    lp = lp_ref[...]
    lam = (jnp.exp(jnp.sum(lp[0:1] * lp[1:2], axis=-1, keepdims=True))
           - jnp.exp(jnp.sum(lp[2:3] * lp[3:4], axis=-1, keepdims=True)) + lam_init)
    lane = lax.broadcasted_iota(jnp.int32, (1, GROUP), 1)
    lane_h = lax.broadcasted_iota(jnp.int32, (1, half), 1)
    n_heads = GROUP // HEAD_DIM

    def attend(lo, nk, n_sub):
        groups = []
        for u in range(n_sub):
            r = slice(u * TILE, (u + 1) * TILE)
            qa = _rope(aq_ref[0, r, :], a_cos_q[r, :], a_sa_q[r, :], a_sb_q[r, :], A_QK // 4)
            qa = qa * (A_QK ** -0.5 * LOG2E)
            q = cq_ref[0, r, :]
            qn = q * lax.rsqrt(_group_mean_sq(q, bd) + RMS_EPS) * qg_ref[...]
            qc = _rope(qn, c_cos_q[r, :], c_sa_q[r, :], c_sb_q[r, :], HEAD_DIM // 4)
            qc = qc * (HEAD_DIM ** -0.5 * LOG2E)
            groups += [(akr_ref, jnp.concatenate([jnp.where(lane // A_QK == 2 * h + c, qa, 0.0).astype(BF16)
                                                  for c in range(2)], axis=0), avt_ref, h)
                       for h in range(n_heads)]
            groups += [(ckr_ref, jnp.concatenate(
                [jnp.where(lane_h // HEAD_DIM == g, qc[:, qb * half:(qb + 1) * half], 0.0).astype(BF16)
                 for qb in range(2)], axis=0), cvt_ref, g) for g in range(2)]
        outs = _attend_t(groups, s_ref, lo, nk)
        per_sub = n_heads + 2
        for u in range(n_sub):
            r = slice(u * TILE, (u + 1) * TILE)
            o = outs[u * per_sub:(u + 1) * per_sub]
            heads = [x[:, 0:TILE] - lam * x[:, TILE:2 * TILE] for x in o[0:n_heads]]
            acc = jnp.concatenate(heads, axis=0).T
            y = acc * lax.rsqrt(_group_mean_sq(acc, bd) + RMS_EPS)
            oa_ref[0, r, :] = y * gain_ref[...] * (1.0 - lam_init)
            heads = [o[n_heads + g][:, qb * TILE:(qb + 1) * TILE] for qb in range(2) for g in range(2)]
            oc_ref[0, r, :] = jnp.concatenate(heads, axis=0).T

    @pl.when(t == 0)
    def _():
        attend(n_tok - tc, tc, 1)

    @pl.when(t > 0)
    def _():
        attend(0, n_tok, aq_ref.shape[1] // TILE)


def _attention(proj, rope_a, rope_c, lam_params, sub_gain, q_gain, k_gain, bd, tc, lam_init):
    b, t, _ = proj.shape
    half = GROUP // 2
    gain = jnp.tile(sub_gain, GROUP // HEAD_DIM).reshape(1, GROUP)
    qg = jnp.tile(q_gain, GROUP // HEAD_DIM).reshape(1, GROUP)
    kg = jnp.tile(k_gain, half // HEAD_DIM).reshape(1, half)

    tq = 2 * TILE
    assert (t - tc) % tq == 0 and tc == TILE
    n_lat = (t - tc) // tq

    def block(j):
        return jnp.where(j == 0, n_lat, j - 1)

    def rows(width, col):
        return pl.BlockSpec((1, tq, width), lambda i, j: (i, block(j), col // width))

    def sample(width, col):
        return pl.BlockSpec((1, t, width), lambda i, j: (i, 0, col // width))

    tile_spec = pl.BlockSpec((tq, GROUP), lambda i, j: (block(j), 0))
    out_spec = pl.BlockSpec((1, tq, GROUP), lambda i, j: (i, block(j), 0))
    return pl.pallas_call(
        functools.partial(_attn_kernel, tc=tc, lam_init=lam_init),
        grid=(b, (t - tc) // tq + 1),
        in_specs=[rows(GROUP, COL_A_Q), sample(GROUP, COL_A_K), sample(GROUP, COL_A_V),
                  rows(GROUP, COL_C_Q), sample(half, COL_C_K), sample(half, COL_C_V),
                  tile_spec, tile_spec, tile_spec,
                  _resident((t, GROUP)), _resident((t, GROUP)), _resident((t, GROUP)),
                  tile_spec, tile_spec, tile_spec,
                  _resident((t, half)), _resident((t, half)), _resident((t, half)),
                  _resident(lam_params.shape), _resident((1, GROUP)), _resident((1, GROUP)),
                  _resident((1, half)), _resident((GROUP, GROUP))],
        out_specs=[out_spec, out_spec],
        out_shape=[jax.ShapeDtypeStruct((b, t, GROUP), F32)] * 2,
        scratch_shapes=[pltpu.VMEM((t, GROUP), BF16),
                        pltpu.VMEM((GROUP // HEAD_DIM, V_ROWS, t), BF16),
                        pltpu.VMEM((t, half), BF16),
                        pltpu.VMEM((half // HEAD_DIM, V_ROWS, t), BF16),
                        pltpu.VMEM((2, t, 2 * TILE), F32)],
        compiler_params=_cparams(2),
        name="attention",
    )(proj, proj, proj, proj, proj, proj, *rope_a, *rope_a, *rope_c, *rope_c,
      lam_params, gain, qg, kg, bd)


def _fourier_stage_kernel(ux_ref, uc_ref, c64_ref, dctx_ref, w_ref, o_ref):
    s = ux_ref.shape[1]
    tc = uc_ref.shape[1]
    w = _dot(ux_ref[0].astype(BF16), c64_ref[...])
    w_ref[0:s, :] = w[:, 0:GROUP].astype(BF16)
    w_ref[s:2 * s, :] = w[:, GROUP:2 * GROUP].astype(BF16)
    wc = _dot(uc_ref[0].astype(BF16), c64_ref[...])
    wc = jnp.concatenate([wc[:, 0:GROUP], wc[:, GROUP:2 * GROUP]], axis=0).astype(BF16)
    o_ref[0] = _dot(dctx_ref[...], wc) * (tc ** -0.5)


def _fourier_dft_kernel(d_ref, w_ref, alias_ref, o_ref):
    del alias_ref
    o_ref[0] = _dot(d_ref[...], w_ref[...]) * ((d_ref.shape[1] // 2) ** -0.5)


def _fourier(proj, c64, dctx, dx, tc):
    b, t, _ = proj.shape
    s = t - tc
    col = COL_D_U // GROUP
    w_all, o_ctx = pl.pallas_call(
        _fourier_stage_kernel,
        grid=(b,),
        in_specs=[pl.BlockSpec((1, s, GROUP), lambda i: (i, 0, col)),
                  pl.BlockSpec((1, tc, GROUP), lambda i: (i, s // tc, col)),
                  pl.BlockSpec((GROUP, 2 * GROUP), lambda i: (0, 0)),
                  pl.BlockSpec((tc, 2 * tc), lambda i: (0, 0))],
        out_specs=[pl.BlockSpec((2 * s, GROUP), lambda i: (0, i)),
                   pl.BlockSpec((1, tc, GROUP), lambda i: (i, s // tc, 0))],
        out_shape=[jax.ShapeDtypeStruct((2 * s, b * GROUP), BF16),
                   jax.ShapeDtypeStruct((b, t, GROUP), F32)],
        compiler_params=_cparams(1),
        name="fourier_stage",
    )(proj, proj, c64, dctx)
    tm = next(c for c in (4 * TILE, 2 * TILE) if s % c == 0)
    return pl.pallas_call(
        _fourier_dft_kernel,
        grid=(s // tm, b),
        in_specs=[pl.BlockSpec((tm, 2 * s), lambda i, j: (i, 0)),
                  pl.BlockSpec((2 * s, GROUP), lambda i, j: (0, j)),
                  pl.BlockSpec(memory_space=pl.ANY)],
        out_specs=pl.BlockSpec((1, tm, GROUP), lambda i, j: (j, i, 0)),
        out_shape=jax.ShapeDtypeStruct((b, t, GROUP), F32),
        input_output_aliases={2: 0},
        compiler_params=_cparams(2),
        name="fourier_dft",
    )(dx, w_all, o_ctx)


def _dft_tables(tc, s):
    def cs_direct(rows, n, period, quarter_turn=False):
        j = jnp.arange(rows, dtype=jnp.int32)
        k = jnp.arange(n, dtype=jnp.int32)
        ang = ((j[:, None] * k[None, :]) % period).astype(F32) * (2.0 * math.pi / period)
        if quarter_turn:
            ang = jnp.concatenate([ang, ang + 0.5 * math.pi], axis=1)
        return jnp.cos(ang), jnp.sin(ang)

    def cos_neg_sin(n):
        ca, sa = cs_direct(n // GRID_W, n, n // GRID_W)
        ca, sa = jnp.tile(ca, (1, 2)), jnp.tile(sa, (1, 2))
        cb, sb = cs_direct(GRID_W, n, n, quarter_turn=True)
        cos = ca[:, None, :] * cb[None, :, :] - sa[:, None, :] * sb[None, :, :]
        return cos.reshape(n, 2 * n).astype(BF16)

    c64, s64 = cs_direct(HEAD_DIM, HEAD_DIM, HEAD_DIM)
    eye = jnp.eye(GROUP // HEAD_DIM, dtype=F32)
    c64 = jnp.concatenate([jnp.kron(eye, c64), jnp.kron(eye, s64)], axis=1) * (HEAD_DIM ** -0.5)
    return c64.astype(BF16), cos_neg_sin(tc), cos_neg_sin(s)


def _hgrn_tables():
    c = CHUNK
    idx = np.arange(c)
    tri = (idx[None, :] <= idx[:, None]).astype(np.float32)
    lvl = np.full((c, c), -1.0, np.float32)
    lvl[idx, idx] = 0.0
    rows = [tri]
    for l in range(1, LEVELS + 1):
        w = c >> (l - 1)
        blk, pos = idx // w, idx % w
        ref = blk * w + w // 2 - 1
        rows.append(tri[ref])
        m = (blk[:, None] == blk[None, :]) & (pos[:, None] >= w // 2) & (pos[None, :] < w // 2)
        lvl[m] = float(l)
    w_f = np.concatenate(rows, axis=0)
    w_b = w_f.reshape(LEVELS + 1, c, c)[:, ::-1, ::-1].reshape(-1, c)
    heads = GROUP // HEAD_DIM
    head_mask = np.kron(np.eye(heads, dtype=np.float32), np.ones((c, HEAD_DIM), np.float32))
    return (np.tile(w_f, (1, 3)), np.tile(w_b, (1, 3)),
            np.tile(lvl, (1, heads)), np.tile(lvl.T, (1, heads)), head_mask)


HGRN_FAST_RANGE = 75.0


def _hgrn_gates(z, lb):
    e = jnp.exp(-jnp.abs(z))
    r = 1.0 / (1.0 + e)
    sig_pos = jnp.where(z >= 0, r, e * r)
    sig_neg = jnp.where(z >= 0, e * r, r)
    return jnp.log(lb + (1.0 - lb) * sig_pos), (1.0 - lb) * sig_neg


def _split3(g):
    g_hi = g.astype(BF16)
    r1 = g - g_hi.astype(F32)
    g_mid = r1.astype(BF16)
    g_lo = (r1 - g_mid.astype(F32)).astype(BF16)
    return jnp.concatenate([g_hi, g_mid, g_lo], axis=0)


def _block_diag(x, hm_b):
    return jnp.concatenate([x.astype(BF16)] * (GROUP // HEAD_DIM), axis=0) * hm_b


def _hgrn_output(ch, a, hm, hm_b, st_ref, end_row):
    q, k, v, b = ch["q"], ch["k"], ch["v"], ch["b"]
    b_end = b[end_row:end_row + 1]
    st = st_ref[...]
    o = _dot(a.astype(BF16), _block_diag(v, hm_b))
    o = o + _dot_nt((q * jnp.exp(b)).astype(BF16), st.astype(BF16))
    upd = _dot_tn(v.astype(BF16), (k * jnp.exp(b_end - b)).astype(BF16))
    st_ref[...] = st * jnp.exp(b_end) + upd * hm
    return o


def _hgrn_scores_fast(ch, lvl, hm_b):
    d = ch["b"] - ch["b_mid"]
    a = _dot_nt((ch["q"] * jnp.exp(d)).astype(BF16), _block_diag(ch["k"] * jnp.exp(-d), hm_b))
    return jnp.where(lvl >= 0.0, a, 0.0)


def _hgrn_scores_levels(ch, w3, lvl, hm_b):
    c = CHUNK
    q, k = ch["q"], ch["k"]
    sums = _dot(w3, ch["g3"])
    b = sums[0:c]
    a = jnp.where(lvl == 0.0, _dot_nt(q.astype(BF16), _block_diag(k, hm_b)), 0.0)
    for l in range(1, LEVELS + 1):
        d = jnp.exp(-jnp.abs(b - sums[l * c:(l + 1) * c]))
        a = jnp.where(lvl == float(l), _dot_nt((q * d).astype(BF16), _block_diag(k * d, hm_b)), a)
    return a


def _hgrn_kernel(qf_ref, zf_ref, vf_ref, qb_ref, zb_ref, vb_ref, lbf_ref, lbb_ref,
                 w3f_ref, w3b_ref, lvf_ref, lvb_ref, hm_ref, of_ref, ob_ref, sf_ref, sb_ref,
                 sf_in_ref, sb_in_ref):
    @pl.when(pl.program_id(1) == 0)
    def _():
        sf_ref[...] = jnp.zeros_like(sf_ref)
        sb_ref[...] = jnp.zeros_like(sb_ref)

    hm = hm_ref[...]
    hm_b = hm.astype(BF16)
    n = TILE // CHUNK
    fwd = dict(q=qf_ref, z=zf_ref, v=vf_ref, lb=lbf_ref, w3=w3f_ref, lvl=lvf_ref, o=of_ref, st=sf_ref,
               st_in=sf_in_ref, end=CHUNK - 1)
    bwd = dict(q=qb_ref, z=zb_ref, v=vb_ref, lb=lbb_ref, w3=w3b_ref, lvl=lvb_ref, o=ob_ref, st=sb_ref,
               st_in=sb_in_ref, end=0)
    order = [(d, ci if d is fwd else n - 1 - ci) for ci in range(n) for d in (fwd, bwd)]

    def chunk(d, ci, w3_rows):
        rows = slice(ci * CHUNK, (ci + 1) * CHUNK)
        g, k = _hgrn_gates(d["z"][0, rows, :], d["lb"][...])
        g3 = _split3(g)
        sums = _dot(d["w3"][0:w3_rows, :], g3)
        return dict(q=d["q"][0, rows, :], v=d["v"][0, rows, :], k=k, g3=g3,
                    b=sums[0:CHUNK], b_mid=sums[CHUNK:2 * CHUNK])

    for d in (fwd, bwd):
        d["st_in"][...] = d["st"][...]
    chunks = [chunk(d, ci, 2 * CHUNK) for d, ci in order]
    spread = None
    for (d, ci), ch in zip(order, chunks):
        dist = jnp.abs(ch["b"] - ch["b_mid"])
        spread = dist if spread is None else jnp.maximum(spread, dist)
        a = _hgrn_scores_fast(ch, d["lvl"][...], hm_b)
        d["o"][0, ci * CHUNK:(ci + 1) * CHUNK, :] = _hgrn_output(ch, a, hm, hm_b, d["st"], d["end"])

    @pl.when(jnp.logical_not(jnp.max(spread) < HGRN_FAST_RANGE))
    def _():
        for d in (fwd, bwd):
            d["st"][...] = d["st_in"][...]
        for d, ci in order:
            ch = chunk(d, ci, 2 * CHUNK)
            a = _hgrn_scores_levels(ch, d["w3"][...], d["lvl"][...], hm_b)
            d["o"][0, ci * CHUNK:(ci + 1) * CHUNK, :] = _hgrn_output(ch, a, hm, hm_b, d["st"], d["end"])


def _hgrn(proj, lb_f, lb_b, tables):
    b, t, _ = proj.shape
    nt = t // TILE
    w3f, w3b, lvf, lvb, hm = tables

    def fwd(col):
        return pl.BlockSpec((1, TILE, GROUP),
                            lambda i, j: (i, jnp.where(j == 0, nt - 1, j - 1), col // GROUP))

    def bwd(col):
        return pl.BlockSpec((1, TILE, GROUP), lambda i, j: (i, nt - 1 - j, col // GROUP))

    def const(a):
        return pl.BlockSpec(a.shape, lambda i, j: (0,) * a.ndim)

    out_f = fwd(0)
    out_b = bwd(0)
    lb_f = lb_f.reshape(1, GROUP)
    lb_b = lb_b.reshape(1, GROUP)
    return pl.pallas_call(
        _hgrn_kernel,
        grid=(b, nt),
        in_specs=[fwd(COL_B_Q), fwd(COL_B_FF), fwd(COL_B_I), bwd(COL_B_Q), bwd(COL_B_FB), bwd(COL_B_I),
                  const(lb_f), const(lb_b), const(w3f), const(w3b), const(lvf), const(lvb), const(hm)],
        out_specs=[out_f, out_b],
        out_shape=[jax.ShapeDtypeStruct((b, t, GROUP), F32)] * 2,
        scratch_shapes=[pltpu.VMEM((GROUP, GROUP), F32)] * 4,
        compiler_params=_cparams(2),
        name="hgrn2",
    )(proj, proj, proj, proj, proj, proj, lb_f, lb_b, w3f, w3b, lvf, lvb, hm)


def _outffn_kernel(*refs, hidden, final, split_input):
    x_ref, ctx_ref = (refs[0], refs[1]) if split_input else (refs[0], None)
    (ma_ref, of_ref, ob_ref, g_ref, mc_ref, md_ref, mod_ref, hg_ref, n2_ref, fn_ref, bd_ref,
     wo_ref, wi_ref, wd_ref, o_ref, act_ref) = refs[-16:]
    n_sub = x_ref.shape[1] // TILE
    o = of_ref[0] + ob_ref[0]
    g = g_ref[0]
    mb = (o * lax.rsqrt(_group_mean_sq(o, bd_ref[...]) + RMS_EPS) * hg_ref[...]) * (g * jax.nn.sigmoid(g))
    mix = jnp.concatenate([ma_ref[0], mb, mc_ref[0], md_ref[0]], axis=-1).astype(BF16)
    y = _dot(mix, wo_ref[...])
    xs, hs, gates = [], [], []
    for i in range(n_sub):
        m = _mod_rows(mod_ref, i, n_sub, not final)
        x = _stream_rows(x_ref, ctx_ref, i, n_sub) + m[2:3] * y[i * TILE:(i + 1) * TILE]
        ms = jnp.mean(x * x, axis=-1, keepdims=True)
        xs.append(x)
        hs.append((x * lax.rsqrt(ms + RMS_EPS) * (n2_ref[...] * (1.0 + m[4:5])) + m[3:4]).astype(BF16))
        gates.append(m[5:6])
    h = jnp.concatenate(hs, axis=0)
    step = 256
    for c in range(hidden // step):
        gate = _dot(h, wi_ref[:, c * step:(c + 1) * step])
        up = _dot(h, wi_ref[:, hidden + c * step:hidden + (c + 1) * step])
        act_ref[:, c * step:(c + 1) * step] = (gate * jax.nn.sigmoid(gate) * up).astype(BF16)
    acc = _dot(act_ref[...], wd_ref[...])
    for i in range(n_sub):
        x = xs[i] + gates[i] * acc[i * TILE:(i + 1) * TILE]
        if final:
            ms = jnp.mean(x * x, axis=-1, keepdims=True)
            x = x * lax.rsqrt(ms + RMS_EPS) * fn_ref[...]
        o_ref[0, i * TILE:(i + 1) * TILE, :] = x


def _out_ffn(xs, ma, o_f, o_b, proj, mc, md, modsel, hgrn_gain, norm2, final_norm, bd, w_out,
             w_ffn_in, w_ffn_out, layer, n_rows, final):
    b, d = modsel.shape[0], modsel.shape[3]
    hidden = w_ffn_out.shape[1]
    tm = _row_tile(n_rows) if not final else 2 * TILE

    def rows(width, col=0):
        return pl.BlockSpec((1, tm, width), lambda i, j: (i, j, col))

    hg = jnp.tile(hgrn_gain, GROUP // HEAD_DIM).reshape(1, GROUP)
    operands, specs = _stream_specs(xs, n_rows, tm)
    return pl.pallas_call(
        functools.partial(_outffn_kernel, hidden=hidden, final=final, split_input=isinstance(xs, tuple)),
        grid=(b, n_rows // tm),
        in_specs=specs + [rows(GROUP), rows(GROUP), rows(GROUP), rows(GROUP, COL_B_G // GROUP),
                          rows(GROUP), rows(GROUP),
                          pl.BlockSpec((1, 2, 6, d), lambda i, j: (i, 0, 0, 0)),
                          _resident((1, GROUP)), _resident((1, d)), _resident((1, d)),
                          _resident((GROUP, GROUP)), _layer_resident(w_out, layer),
                          _layer_resident(w_ffn_in, layer), _layer_resident(w_ffn_out, layer)],
        out_specs=rows(d),
        out_shape=jax.ShapeDtypeStruct((b, n_rows, d), F32),
        scratch_shapes=[pltpu.VMEM((tm, hidden), BF16)],
        compiler_params=_cparams(2),
        name="out_ffn",
    )(*operands, ma, o_f, o_b, proj, mc, md, modsel, hg, norm2.reshape(1, d), final_norm.reshape(1, d),
      bd, w_out, w_ffn_in, w_ffn_out)


def _rope_tables(tc, s, dim, width):
    n_freq = dim // 4
    inv_freq = ROPE_THETA ** (-jnp.arange(n_freq, dtype=F32) / n_freq)
    lane = np.arange(width) % dim
    is_col = (lane // (dim // 2)) == 1
    second = ((lane % (dim // 2)) // n_freq) == 1
    freq = inv_freq[lane % n_freq]
    ang_r = jnp.arange(s // GRID_W, dtype=F32)[:, None] * freq[None, :]
    ang_c = jnp.arange(GRID_W, dtype=F32)[:, None] * freq[None, :]

    def on_grid(f):
        return jnp.where(is_col[None, None, :], f(ang_c)[None, :, :], f(ang_r)[:, None, :]).reshape(s, width)

    cos, sin = on_grid(jnp.cos), on_grid(jnp.sin)
    sin_a = jnp.where(second[None, :], sin, 0.0)
    sin_b = jnp.where(second[None, :], 0.0, -sin)
    ident = jnp.ones((tc, width), F32)
    zero = jnp.zeros((tc, width), F32)
    return (jnp.concatenate([cos, ident], axis=0), jnp.concatenate([sin_a, zero], axis=0),
            jnp.concatenate([sin_b, zero], axis=0))


def kernel(x, c, ctx, c_ctx, w_mod, b_mod, norm1, w_in, diff_lambda, diff_norm, hgrn_lb_logits,
           hgrn_norm, q_norm, k_norm, w_out, norm2, w_ffn_in, w_ffn_out, final_norm):
    b, s, d = x.shape
    tc = ctx.shape[1]
    depth = w_mod.shape[0]
    assert tc == TILE and s % TILE == 0 and s % GRID_W == 0 and d == 4 * GROUP

    rope_a = _rope_tables(tc, s, A_QK, GROUP)
    rope_c = _rope_tables(tc, s, HEAD_DIM, GROUP)
    dft = _dft_tables(tc, s)
    hgrn_tabs = _hgrn_tables()
    hgrn_tabs = tuple(jnp.asarray(a, BF16 if i < 2 else F32) for i, a in enumerate(hgrn_tabs))
    bd = jnp.asarray(np.kron(np.eye(GROUP // HEAD_DIM), np.full((HEAD_DIM, HEAD_DIM), 1.0 / HEAD_DIM)), BF16)

    p = jax.nn.softmax(hgrn_lb_logits.astype(F32), axis=1)
    lower = jnp.cumsum(p, axis=1) - p[:, :1]

    def swap_heads(w, axis, start):
        cuts = [0, start + HEAD_DIM, start + 2 * HEAD_DIM, start + 3 * HEAD_DIM, w.shape[axis]]
        parts = [lax.slice_in_dim(w, cuts[i], cuts[i + 1], axis=axis) for i in range(4)]
        return jnp.concatenate([parts[0], parts[2], parts[1], parts[3]], axis=axis).astype(BF16)

    w_in_b = swap_heads(w_in, 2, COL_C_Q)
    w_out_b = swap_heads(w_out, 1, 2 * GROUP)
    w_ffn_in_b = w_ffn_in.astype(BF16)
    w_ffn_out_b = w_ffn_out.astype(BF16)

    rows = -(-(b + 1) // 8) * 8
    cc = jnp.zeros((rows, d), F32).at[:b].set(c).at[b].set(c_ctx)
    mod = _modulation(cc, w_mod, b_mod).reshape(depth, rows, 6, d)
    modsel = jnp.stack([jnp.broadcast_to(mod[:, b:b + 1], (depth, b, 6, d)), mod[:, :b]], axis=2)

    xs = (x, ctx)
    for l in range(depth):
        last = l == depth - 1
        lam_init = 0.8 - 0.6 * math.exp(-0.3 * l)
        proj = _in_proj(xs, s + tc, modsel[l], norm1[l], w_in_b, l)
        m_a, m_c = _attention(proj, rope_a, rope_c, diff_lambda[l], diff_norm[l], q_norm[l], k_norm[l],
                              bd, tc, lam_init)
        o_f, o_b = _hgrn(proj, lower[0, l], lower[1, l], hgrn_tabs)
        m_d = _fourier(proj, *dft, tc)
        if last and isinstance(xs, tuple):
            xs = xs[0]
        xs = _out_ffn(xs, m_a, o_f, o_b, proj, m_c, m_d, modsel[l], hgrn_norm[l], norm2[l], final_norm,
                      bd, w_out_b, w_ffn_in_b, w_ffn_out_b, l, s if last else s + tc, last)
    return xs
```

```python
import functools
import math

import numpy as np
import jax
import jax.numpy as jnp
from jax import lax
from jax.experimental import pallas as pl
from jax.experimental.pallas import tpu as pltpu

F32 = jnp.float32
BF16 = jnp.bfloat16

HEAD_DIM = 64
GRID_W = 64
ROPE_THETA = 10000.0
RMS_EPS = 1e-6
GROUP = 256
A_QK = 32
TILE = 256
CHUNK = 64
LEVELS = 6
LOG2E = 1.4426950408889634
VMEM_LIMIT = 56 * 1024 * 1024

COL_A_Q, COL_A_K, COL_A_V = 0, 256, 512
COL_B_Q, COL_B_FF, COL_B_FB, COL_B_I, COL_B_G = 768, 1024, 1280, 1536, 1792
COL_C_Q, COL_C_K, COL_C_V = 2048, 2304, 2432
COL_D_U = 2560


def _cparams(n_axes):
    return pltpu.CompilerParams(dimension_semantics=("arbitrary",) * n_axes,
                                vmem_limit_bytes=VMEM_LIMIT)


def _dot(a, b):
    return jnp.dot(a, b, preferred_element_type=F32)


def _dot_nt(a, b):
    return lax.dot_general(a, b, (((1,), (1,)), ((), ())), preferred_element_type=F32)


def _dot_tn(a, b):
    return lax.dot_general(a, b, (((0,), (0,)), ((), ())), preferred_element_type=F32)


def _group_mean_sq(y, bd):
    y2 = y * y
    hi = y2.astype(BF16)
    lo = (y2 - hi.astype(F32)).astype(BF16)
    return _dot(hi, bd) + _dot(lo, bd)


def _rope(x, cos, sin_a, sin_b, shift):
    w = x.shape[-1]
    return x * cos + pltpu.roll(x, shift, 1) * sin_a + pltpu.roll(x, w - shift, 1) * sin_b


def _mod_kernel(c_ref, w_ref, b_ref, o_ref):
    c = c_ref[...]
    a = c * jax.nn.sigmoid(c)
    o_ref[0] = _dot(a.astype(BF16), w_ref[0].astype(BF16)) + b_ref[0]


def _modulation(cc, w_mod, b_mod):
    depth, d, n = w_mod.shape
    r = cc.shape[0]
    tn = 1536
    return pl.pallas_call(
        _mod_kernel,
        grid=(depth, n // tn),
        in_specs=[pl.BlockSpec((r, d), lambda l, j: (0, 0)),
                  pl.BlockSpec((1, d, tn), lambda l, j: (l, 0, j)),
                  pl.BlockSpec((1, 1, tn), lambda l, j: (l, 0, j))],
        out_specs=pl.BlockSpec((1, r, tn), lambda l, j: (l, 0, j)),
        out_shape=jax.ShapeDtypeStruct((depth, r, n), F32),
        compiler_params=_cparams(2),
        name="modulation",
    )(cc, w_mod, b_mod.reshape(depth, 1, n))


def _mod_rows(mod_ref, sub, n_sub, with_ctx):
    m = mod_ref[0]
    if not (with_ctx and sub == n_sub - 1):
        return m[1]
    return jnp.where(pl.program_id(1) == pl.num_programs(1) - 1, m[0], m[1])


def _stream_rows(x_ref, ctx_ref, sub, n_sub):
    rows = slice(sub * TILE, (sub + 1) * TILE)
    if ctx_ref is None or sub < n_sub - 1:
        return x_ref[0, rows, :]
    return lax.cond(pl.program_id(1) == pl.num_programs(1) - 1,
                    lambda: ctx_ref[0], lambda: x_ref[0, rows, :])


def _inproj_kernel(*refs, split_input):
    x_ref, ctx_ref = (refs[0], refs[1]) if split_input else (refs[0], None)
    mod_ref, n1_ref, w_ref, o_ref = refs[-4:]
    n_sub = x_ref.shape[1] // TILE
    for i in range(n_sub):
        x = _stream_rows(x_ref, ctx_ref, i, n_sub)
        ms = jnp.mean(x * x, axis=-1, keepdims=True)
        m = _mod_rows(mod_ref, i, n_sub, True)
        h = x * lax.rsqrt(ms + RMS_EPS) * (n1_ref[...] * (1.0 + m[1:2])) + m[0:1]
        o_ref[0, i * TILE:(i + 1) * TILE, :] = _dot(h.astype(BF16), w_ref[...]).astype(o_ref.dtype)


def _row_tile(t):
    return next(tm for tm in (3 * TILE, 2 * TILE, TILE) if t % tm == 0)


def _resident(shape):
    return pl.BlockSpec(shape, lambda i, j: (0,) * len(shape), pipeline_mode=pl.Buffered(1))


def _layer_resident(w, layer):
    return pl.BlockSpec((None,) + w.shape[1:], lambda i, j: (layer, 0, 0), pipeline_mode=pl.Buffered(1))


def _stream_specs(stream, t, tm):
    if not isinstance(stream, tuple):
        return (stream,), [pl.BlockSpec((1, tm, stream.shape[2]), lambda i, j: (i, j, 0))]
    x, ctx = stream
    assert x.shape[1] + ctx.shape[1] == t and ctx.shape[1] == TILE and t % tm == 0
    last = (x.shape[1] - 1) // tm
    return stream, [pl.BlockSpec((1, tm, x.shape[2]), lambda i, j: (i, jnp.minimum(j, last), 0)),
                    pl.BlockSpec((1, TILE, x.shape[2]), lambda i, j: (i, 0, 0))]


def _in_proj(stream, t, modsel, norm1, w_in, layer):
    b, d = modsel.shape[0], modsel.shape[3]
    n = w_in.shape[2]
    tm = _row_tile(t)
    operands, specs = _stream_specs(stream, t, tm)
    return pl.pallas_call(
        functools.partial(_inproj_kernel, split_input=isinstance(stream, tuple)),
        grid=(b, t // tm),
        in_specs=specs + [pl.BlockSpec((1, 2, 6, d), lambda i, j: (i, 0, 0, 0)),
                          _resident((1, d)), _layer_resident(w_in, layer)],
        out_specs=pl.BlockSpec((1, tm, n), lambda i, j: (i, j, 0)),
        out_shape=jax.ShapeDtypeStruct((b, t, n), BF16),
        compiler_params=_cparams(2),
        name="in_proj",
    )(*operands, modsel, norm1.reshape(1, d), w_in)


V_ROWS = HEAD_DIM + 16


KEY_TILES = (768, 256)


def _attend_t(groups, s_ref, lo, nk):
    key_tile = next(c for c in KEY_TILES if nk % c == 0)
    n_kt = nk // key_tile
    n_g = len(groups)
    maxes = [None] * n_g
    accs = [None] * n_g

    def keys(kt):
        return slice(lo + kt * key_tile, lo + (kt + 1) * key_tile)

    def score_tile(g, kt):
        kr_ref, wq, _, _ = groups[g]
        s = _dot_nt(kr_ref[keys(kt), :], wq)
        s_ref[g % 2, kt * key_tile:(kt + 1) * key_tile, :] = s
        m = jnp.max(s, axis=0, keepdims=True)
        maxes[g] = m if kt == 0 else jnp.maximum(maxes[g], m)

    def value_tile(g, kt, p):
        _, _, vt_ref, head = groups[g]
        pv = _dot(vt_ref[head, :, keys(kt)], p)
        accs[g] = pv if kt == 0 else accs[g] + pv

    for kt in range(n_kt):
        score_tile(0, kt)
    for g in range(n_g):
        p_prev = None
        for kt in range(n_kt):
            if g + 1 < n_g:
                score_tile(g + 1, kt)
            p = jnp.exp2(s_ref[g % 2, kt * key_tile:(kt + 1) * key_tile, :] - maxes[g]).astype(BF16)
            if kt > 0:
                value_tile(g, kt - 1, p_prev)
            p_prev = p
        value_tile(g, n_kt - 1, p_prev)
    return [a[0:HEAD_DIM] * (1.0 / a[HEAD_DIM:HEAD_DIM + 1]) for a in accs]


def _store_values_t(vt_ref, v):
    v_t = v.T
    for h in range(vt_ref.shape[0]):
        vt_ref[h, 0:HEAD_DIM, :] = v_t[h * HEAD_DIM:(h + 1) * HEAD_DIM, :].astype(BF16)
        vt_ref[h, HEAD_DIM:V_ROWS, :] = jnp.ones((V_ROWS - HEAD_DIM, v.shape[0]), BF16)


def _attn_kernel(aq_ref, ak_ref, av_ref, cq_ref, ck_ref, cv_ref,
                 a_cos_q, a_sa_q, a_sb_q, a_cos_k, a_sa_k, a_sb_k,
                 c_cos_q, c_sa_q, c_sb_q, c_cos_k, c_sa_k, c_sb_k,
                 lp_ref, gain_ref, qg_ref, kg_ref, bd_ref, oa_ref, oc_ref,
                 akr_ref, avt_ref, ckr_ref, cvt_ref, s_ref, *, tc, lam_init):
    t = pl.program_id(1)
    n_tok = akr_ref.shape[0]
    half = GROUP // 2
    bd = bd_ref[...]

    @pl.when(t == 0)
    def _():
        kr = _rope(ak_ref[0].astype(F32), a_cos_k[...], a_sa_k[...], a_sb_k[...], A_QK // 4)
        akr_ref[...] = kr.astype(BF16)
        _store_values_t(avt_ref, av_ref[0].astype(F32))
        k = ck_ref[0].astype(F32)
        kn = k * lax.rsqrt(_group_mean_sq(k, bd[0:half, 0:half]) + RMS_EPS) * kg_ref[...]
        ckr_ref[...] = _rope(kn, c_cos_k[...], c_sa_k[...], c_sb_k[...], HEAD_DIM // 4).astype(BF16)
        _store_values_t(cvt_ref, cv_ref[0].astype(F32))

    lp = lp_ref[...]
    lam = (jnp.exp(jnp.sum(lp[0:1] * lp[1:2], axis=-1, keepdims=True))
           - jnp.exp(jnp.sum(lp[2:3] * lp[3:4], axis=-1, keepdims=True)) + lam_init)
    lane = lax.broadcasted_iota(jnp.int32, (1, GROUP), 1)
    lane_h = lax.broadcasted_iota(jnp.int32, (1, half), 1)
    n_heads = GROUP // HEAD_DIM

    def attend(lo, nk, n_sub):
        groups = []
        for u in range(n_sub):
            r = slice(u * TILE, (u + 1) * TILE)
            qa = _rope(aq_ref[0, r, :].astype(F32), a_cos_q[r, :], a_sa_q[r, :], a_sb_q[r, :], A_QK // 4)
            qa = qa * (A_QK ** -0.5 * LOG2E)
            q = cq_ref[0, r, :].astype(F32)
            qn = q * lax.rsqrt(_group_mean_sq(q, bd) + RMS_EPS) * qg_ref[...]
            qc = _rope(qn, c_cos_q[r, :], c_sa_q[r, :], c_sb_q[r, :], HEAD_DIM // 4)
            qc = qc * (HEAD_DIM ** -0.5 * LOG2E)
            groups += [(akr_ref, jnp.concatenate([jnp.where(lane // A_QK == 2 * h + c, qa, 0.0).astype(BF16)
                                                  for c in range(2)], axis=0), avt_ref, h)
                       for h in range(n_heads)]
            groups += [(ckr_ref, jnp.concatenate(
                [jnp.where(lane_h // HEAD_DIM == g, qc[:, qb * half:(qb + 1) * half], 0.0).astype(BF16)
                 for qb in range(2)], axis=0), cvt_ref, g) for g in range(2)]
        outs = _attend_t(groups, s_ref, lo, nk)
        per_sub = n_heads + 2
        for u in range(n_sub):
            r = slice(u * TILE, (u + 1) * TILE)
            o = outs[u * per_sub:(u + 1) * per_sub]
            heads = [x[:, 0:TILE] - lam * x[:, TILE:2 * TILE] for x in o[0:n_heads]]
            acc = jnp.concatenate(heads, axis=0).T
            y = acc * lax.rsqrt(_group_mean_sq(acc, bd) + RMS_EPS)
            oa_ref[0, r, :] = y * gain_ref[...] * (1.0 - lam_init)
            heads = [o[n_heads + g][:, qb * TILE:(qb + 1) * TILE] for qb in range(2) for g in range(2)]
            oc_ref[0, r, :] = jnp.concatenate(heads, axis=0).T

    @pl.when(t == 0)
    def _():
        attend(n_tok - tc, tc, 1)

    @pl.when(t > 0)
    def _():
        attend(0, n_tok, aq_ref.shape[1] // TILE)


def _attention(proj, rope_a, rope_c, lam_params, sub_gain, q_gain, k_gain, bd, tc, lam_init):
    b, t, _ = proj.shape
    half = GROUP // 2
    gain = jnp.tile(sub_gain, GROUP // HEAD_DIM).reshape(1, GROUP)
    qg = jnp.tile(q_gain, GROUP // HEAD_DIM).reshape(1, GROUP)
    kg = jnp.tile(k_gain, half // HEAD_DIM).reshape(1, half)

    tq = 2 * TILE
    assert (t - tc) % tq == 0 and tc == TILE
    n_lat = (t - tc) // tq

    def block(j):
        return jnp.where(j == 0, n_lat, j - 1)

    def rows(width, col):
        return pl.BlockSpec((1, tq, width), lambda i, j: (i, block(j), col // width))

    def sample(width, col):
        return pl.BlockSpec((1, t, width), lambda i, j: (i, 0, col // width))

    tile_spec = pl.BlockSpec((tq, GROUP), lambda i, j: (block(j), 0))
    out_spec = pl.BlockSpec((1, tq, GROUP), lambda i, j: (i, block(j), 0))
    return pl.pallas_call(
        functools.partial(_attn_kernel, tc=tc, lam_init=lam_init),
        grid=(b, (t - tc) // tq + 1),
        in_specs=[rows(GROUP, COL_A_Q), sample(GROUP, COL_A_K), sample(GROUP, COL_A_V),
                  rows(GROUP, COL_C_Q), sample(half, COL_C_K), sample(half, COL_C_V),
                  tile_spec, tile_spec, tile_spec,
                  _resident((t, GROUP)), _resident((t, GROUP)), _resident((t, GROUP)),
                  tile_spec, tile_spec, tile_spec,
                  _resident((t, half)), _resident((t, half)), _resident((t, half)),
                  _resident(lam_params.shape), _resident((1, GROUP)), _resident((1, GROUP)),
                  _resident((1, half)), _resident((GROUP, GROUP))],
        out_specs=[out_spec, out_spec],
        out_shape=[jax.ShapeDtypeStruct((b, t, GROUP), F32)] * 2,
        scratch_shapes=[pltpu.VMEM((t, GROUP), BF16),
                        pltpu.VMEM((GROUP // HEAD_DIM, V_ROWS, t), BF16),
                        pltpu.VMEM((t, half), BF16),
                        pltpu.VMEM((half // HEAD_DIM, V_ROWS, t), BF16),
                        pltpu.VMEM((2, t, 2 * TILE), F32)],
        compiler_params=_cparams(2),
        name="attention",
    )(proj, proj, proj, proj, proj, proj, *rope_a, *rope_a, *rope_c, *rope_c,
      lam_params, gain, qg, kg, bd)


def _fourier_stage_kernel(ux_ref, uc_ref, c64_ref, dctx_ref, w_ref, o_ref):
    s = ux_ref.shape[1]
    tc = uc_ref.shape[1]
    w = _dot(ux_ref[0].astype(BF16), c64_ref[...])
    w_ref[0:s, :] = w[:, 0:GROUP].astype(BF16)
    w_ref[s:2 * s, :] = w[:, GROUP:2 * GROUP].astype(BF16)
    wc = _dot(uc_ref[0].astype(BF16), c64_ref[...])
    wc = jnp.concatenate([wc[:, 0:GROUP], wc[:, GROUP:2 * GROUP]], axis=0).astype(BF16)
    o_ref[0] = _dot(dctx_ref[...], wc) * (tc ** -0.5)


def _fourier_dft_kernel(d_ref, w_ref, alias_ref, o_ref):
    del alias_ref
    o_ref[0] = _dot(d_ref[...], w_ref[...]) * ((d_ref.shape[1] // 2) ** -0.5)


def _fourier(proj, c64, dctx, dx, tc):
    b, t, _ = proj.shape
    s = t - tc
    col = COL_D_U // GROUP
    w_all, o_ctx = pl.pallas_call(
        _fourier_stage_kernel,
        grid=(b,),
        in_specs=[pl.BlockSpec((1, s, GROUP), lambda i: (i, 0, col)),
                  pl.BlockSpec((1, tc, GROUP), lambda i: (i, s // tc, col)),
                  pl.BlockSpec((GROUP, 2 * GROUP), lambda i: (0, 0)),
                  pl.BlockSpec((tc, 2 * tc), lambda i: (0, 0))],
        out_specs=[pl.BlockSpec((2 * s, GROUP), lambda i: (0, i)),
                   pl.BlockSpec((1, tc, GROUP), lambda i: (i, s // tc, 0))],
        out_shape=[jax.ShapeDtypeStruct((2 * s, b * GROUP), BF16),
                   jax.ShapeDtypeStruct((b, t, GROUP), F32)],
        compiler_params=_cparams(1),
        name="fourier_stage",
    )(proj, proj, c64, dctx)
    tm = next(c for c in (4 * TILE, 2 * TILE) if s % c == 0)
    return pl.pallas_call(
        _fourier_dft_kernel,
        grid=(s // tm, b),
        in_specs=[pl.BlockSpec((tm, 2 * s), lambda i, j: (i, 0)),
                  pl.BlockSpec((2 * s, GROUP), lambda i, j: (0, j)),
                  pl.BlockSpec(memory_space=pl.ANY)],
        out_specs=pl.BlockSpec((1, tm, GROUP), lambda i, j: (j, i, 0)),
        out_shape=jax.ShapeDtypeStruct((b, t, GROUP), F32),
        input_output_aliases={2: 0},
        compiler_params=_cparams(2),
        name="fourier_dft",
    )(dx, w_all, o_ctx)


def _dft_tables(tc, s):
    def cs_direct(rows, n, period, quarter_turn=False):
        j = jnp.arange(rows, dtype=jnp.int32)
        k = jnp.arange(n, dtype=jnp.int32)
        ang = ((j[:, None] * k[None, :]) % period).astype(F32) * (2.0 * math.pi / period)
        if quarter_turn:
            ang = jnp.concatenate([ang, ang + 0.5 * math.pi], axis=1)
        return jnp.cos(ang), jnp.sin(ang)

    def cos_neg_sin(n):
        ca, sa = cs_direct(n // GRID_W, n, n // GRID_W)
        ca, sa = jnp.tile(ca, (1, 2)), jnp.tile(sa, (1, 2))
        cb, sb = cs_direct(GRID_W, n, n, quarter_turn=True)
        cos = ca[:, None, :] * cb[None, :, :] - sa[:, None, :] * sb[None, :, :]
        return cos.reshape(n, 2 * n).astype(BF16)

    c64, s64 = cs_direct(HEAD_DIM, HEAD_DIM, HEAD_DIM)
    eye = jnp.eye(GROUP // HEAD_DIM, dtype=F32)
    c64 = jnp.concatenate([jnp.kron(eye, c64), jnp.kron(eye, s64)], axis=1) * (HEAD_DIM ** -0.5)
    return c64.astype(BF16), cos_neg_sin(tc), cos_neg_sin(s)


def _hgrn_tables():
    c = CHUNK
    idx = np.arange(c)
    tri = (idx[None, :] <= idx[:, None]).astype(np.float32)
    lvl = np.full((c, c), -1.0, np.float32)
    lvl[idx, idx] = 0.0
    rows = [tri]
    for l in range(1, LEVELS + 1):
        w = c >> (l - 1)
        blk, pos = idx // w, idx % w
        ref = blk * w + w // 2 - 1
        rows.append(tri[ref])
        m = (blk[:, None] == blk[None, :]) & (pos[:, None] >= w // 2) & (pos[None, :] < w // 2)
        lvl[m] = float(l)
    w_f = np.concatenate(rows, axis=0)
    w_b = w_f.reshape(LEVELS + 1, c, c)[:, ::-1, ::-1].reshape(-1, c)
    heads = GROUP // HEAD_DIM
    head_mask = np.kron(np.eye(heads, dtype=np.float32), np.ones((c, HEAD_DIM), np.float32))
    return (np.tile(w_f, (1, 3)), np.tile(w_b, (1, 3)),
            np.tile(lvl, (1, heads)), np.tile(lvl.T, (1, heads)), head_mask)


HGRN_FAST_RANGE = 75.0


def _hgrn_gates(z, lb):
    e = jnp.exp(-jnp.abs(z))
    r = 1.0 / (1.0 + e)
    sig_pos = jnp.where(z >= 0, r, e * r)
    sig_neg = jnp.where(z >= 0, e * r, r)
    return jnp.log(lb + (1.0 - lb) * sig_pos), (1.0 - lb) * sig_neg


def _split3(g):
    g_hi = g.astype(BF16)
    r1 = g - g_hi.astype(F32)
    g_mid = r1.astype(BF16)
    g_lo = (r1 - g_mid.astype(F32)).astype(BF16)
    return jnp.concatenate([g_hi, g_mid, g_lo], axis=0)


def _block_diag(x, hm_b):
    return jnp.concatenate([x.astype(BF16)] * (GROUP // HEAD_DIM), axis=0) * hm_b


def _hgrn_output(ch, a, hm, hm_b, st_ref, end_row):
    q, k, v, b = ch["q"], ch["k"], ch["v"], ch["b"]
    b_end = b[end_row:end_row + 1]
    st = st_ref[...]
    o = _dot(a.astype(BF16), _block_diag(v, hm_b))
    o = o + _dot_nt((q * jnp.exp(b)).astype(BF16), st.astype(BF16))
    upd = _dot_tn(v.astype(BF16), (k * jnp.exp(b_end - b)).astype(BF16))
    st_ref[...] = st * jnp.exp(b_end) + upd * hm
    return o


def _hgrn_scores_fast(ch, lvl, hm_b):
    d = ch["b"] - ch["b_mid"]
    a = _dot_nt((ch["q"] * jnp.exp(d)).astype(BF16), _block_diag(ch["k"] * jnp.exp(-d), hm_b))
    return jnp.where(lvl >= 0.0, a, 0.0)


def _hgrn_scores_levels(ch, w3, lvl, hm_b):
    c = CHUNK
    q, k = ch["q"], ch["k"]
    sums = _dot(w3, ch["g3"])
    b = sums[0:c]
    a = jnp.where(lvl == 0.0, _dot_nt(q.astype(BF16), _block_diag(k, hm_b)), 0.0)
    for l in range(1, LEVELS + 1):
        d = jnp.exp(-jnp.abs(b - sums[l * c:(l + 1) * c]))
        a = jnp.where(lvl == float(l), _dot_nt((q * d).astype(BF16), _block_diag(k * d, hm_b)), a)
    return a


def _hgrn_kernel(qf_ref, zf_ref, vf_ref, qb_ref, zb_ref, vb_ref, lbf_ref, lbb_ref,
                 w3f_ref, w3b_ref, lvf_ref, lvb_ref, hm_ref, of_ref, ob_ref, sf_ref, sb_ref,
                 sf_in_ref, sb_in_ref):
    @pl.when(pl.program_id(1) == 0)
    def _():
        sf_ref[...] = jnp.zeros_like(sf_ref)
        sb_ref[...] = jnp.zeros_like(sb_ref)

    hm = hm_ref[...]
    hm_b = hm.astype(BF16)
    n = TILE // CHUNK
    fwd = dict(q=qf_ref, z=zf_ref, v=vf_ref, lb=lbf_ref, w3=w3f_ref, lvl=lvf_ref, o=of_ref, st=sf_ref,
               st_in=sf_in_ref, end=CHUNK - 1)
    bwd = dict(q=qb_ref, z=zb_ref, v=vb_ref, lb=lbb_ref, w3=w3b_ref, lvl=lvb_ref, o=ob_ref, st=sb_ref,
               st_in=sb_in_ref, end=0)
    order = [(d, ci if d is fwd else n - 1 - ci) for ci in range(n) for d in (fwd, bwd)]

    def chunk(d, ci, w3_rows):
        rows = slice(ci * CHUNK, (ci + 1) * CHUNK)
        g, k = _hgrn_gates(d["z"][0, rows, :].astype(F32), d["lb"][...])
        g3 = _split3(g)
        sums = _dot(d["w3"][0:w3_rows, :], g3)
        return dict(q=d["q"][0, rows, :].astype(F32), v=d["v"][0, rows, :].astype(F32), k=k, g3=g3,
                    b=sums[0:CHUNK], b_mid=sums[CHUNK:2 * CHUNK])

    for d in (fwd, bwd):
        d["st_in"][...] = d["st"][...]
    chunks = [chunk(d, ci, 2 * CHUNK) for d, ci in order]
    spread = None
    for (d, ci), ch in zip(order, chunks):
        dist = jnp.abs(ch["b"] - ch["b_mid"])
        spread = dist if spread is None else jnp.maximum(spread, dist)
        a = _hgrn_scores_fast(ch, d["lvl"][...], hm_b)
        d["o"][0, ci * CHUNK:(ci + 1) * CHUNK, :] = _hgrn_output(ch, a, hm, hm_b, d["st"], d["end"])

    @pl.when(jnp.logical_not(jnp.max(spread) < HGRN_FAST_RANGE))
    def _():
        for d in (fwd, bwd):
            d["st"][...] = d["st_in"][...]
        for d, ci in order:
            ch = chunk(d, ci, 2 * CHUNK)
            a = _hgrn_scores_levels(ch, d["w3"][...], d["lvl"][...], hm_b)
            d["o"][0, ci * CHUNK:(ci + 1) * CHUNK, :] = _hgrn_output(ch, a, hm, hm_b, d["st"], d["end"])


def _hgrn(proj, lb_f, lb_b, tables):
    b, t, _ = proj.shape
    nt = t // TILE
    w3f, w3b, lvf, lvb, hm = tables

    def fwd(col):
        return pl.BlockSpec((1, TILE, GROUP),
                            lambda i, j: (i, jnp.where(j == 0, nt - 1, j - 1), col // GROUP))

    def bwd(col):
        return pl.BlockSpec((1, TILE, GROUP), lambda i, j: (i, nt - 1 - j, col // GROUP))

    def const(a):
        return pl.BlockSpec(a.shape, lambda i, j: (0,) * a.ndim)

    out_f = fwd(0)
    out_b = bwd(0)
    lb_f = lb_f.reshape(1, GROUP)
    lb_b = lb_b.reshape(1, GROUP)
    return pl.pallas_call(
        _hgrn_kernel,
        grid=(b, nt),
        in_specs=[fwd(COL_B_Q), fwd(COL_B_FF), fwd(COL_B_I), bwd(COL_B_Q), bwd(COL_B_FB), bwd(COL_B_I),
                  const(lb_f), const(lb_b), const(w3f), const(w3b), const(lvf), const(lvb), const(hm)],
        out_specs=[out_f, out_b],
        out_shape=[jax.ShapeDtypeStruct((b, t, GROUP), F32)] * 2,
        scratch_shapes=[pltpu.VMEM((GROUP, GROUP), F32)] * 4,
        compiler_params=_cparams(2),
        name="hgrn2",
    )(proj, proj, proj, proj, proj, proj, lb_f, lb_b, w3f, w3b, lvf, lvb, hm)


def _outffn_kernel(*refs, hidden, final, split_input):
    x_ref, ctx_ref = (refs[0], refs[1]) if split_input else (refs[0], None)
    (ma_ref, of_ref, ob_ref, g_ref, mc_ref, md_ref, mod_ref, hg_ref, n2_ref, fn_ref, bd_ref,
     wo_ref, wi_ref, wd_ref, o_ref, act_ref) = refs[-16:]
    n_sub = x_ref.shape[1] // TILE
    o = of_ref[0] + ob_ref[0]
    g = g_ref[0].astype(F32)
    mb = (o * lax.rsqrt(_group_mean_sq(o, bd_ref[...]) + RMS_EPS) * hg_ref[...]) * (g * jax.nn.sigmoid(g))
    mix = jnp.concatenate([ma_ref[0], mb, mc_ref[0], md_ref[0]], axis=-1).astype(BF16)
    y = _dot(mix, wo_ref[...])
    xs, hs, gates = [], [], []
    for i in range(n_sub):
        m = _mod_rows(mod_ref, i, n_sub, not final)
        x = _stream_rows(x_ref, ctx_ref, i, n_sub) + m[2:3] * y[i * TILE:(i + 1) * TILE]
        ms = jnp.mean(x * x, axis=-1, keepdims=True)
        xs.append(x)
        hs.append((x * lax.rsqrt(ms + RMS_EPS) * (n2_ref[...] * (1.0 + m[4:5])) + m[3:4]).astype(BF16))
        gates.append(m[5:6])
    h = jnp.concatenate(hs, axis=0)
    step = 256
    for c in range(hidden // step):
        gate = _dot(h, wi_ref[:, c * step:(c + 1) * step])
        up = _dot(h, wi_ref[:, hidden + c * step:hidden + (c + 1) * step])
        act_ref[:, c * step:(c + 1) * step] = (gate * jax.nn.sigmoid(gate) * up).astype(BF16)
    acc = _dot(act_ref[...], wd_ref[...])
    for i in range(n_sub):
        x = xs[i] + gates[i] * acc[i * TILE:(i + 1) * TILE]
        if final:
            ms = jnp.mean(x * x, axis=-1, keepdims=True)
            x = x * lax.rsqrt(ms + RMS_EPS) * fn_ref[...]
        o_ref[0, i * TILE:(i + 1) * TILE, :] = x


def _out_ffn(xs, ma, o_f, o_b, proj, mc, md, modsel, hgrn_gain, norm2, final_norm, bd, w_out,
             w_ffn_in, w_ffn_out, layer, n_rows, final):
    b, d = modsel.shape[0], modsel.shape[3]
    hidden = w_ffn_out.shape[1]
    tm = _row_tile(n_rows) if not final else 2 * TILE

    def rows(width, col=0):
        return pl.BlockSpec((1, tm, width), lambda i, j: (i, j, col))

    hg = jnp.tile(hgrn_gain, GROUP // HEAD_DIM).reshape(1, GROUP)
    operands, specs = _stream_specs(xs, n_rows, tm)
    return pl.pallas_call(
        functools.partial(_outffn_kernel, hidden=hidden, final=final, split_input=isinstance(xs, tuple)),
        grid=(b, n_rows // tm),
        in_specs=specs + [rows(GROUP), rows(GROUP), rows(GROUP), rows(GROUP, COL_B_G // GROUP),
                          rows(GROUP), rows(GROUP),
                          pl.BlockSpec((1, 2, 6, d), lambda i, j: (i, 0, 0, 0)),
                          _resident((1, GROUP)), _resident((1, d)), _resident((1, d)),
                          _resident((GROUP, GROUP)), _layer_resident(w_out, layer),
                          _layer_resident(w_ffn_in, layer), _layer_resident(w_ffn_out, layer)],
        out_specs=rows(d),
        out_shape=jax.ShapeDtypeStruct((b, n_rows, d), F32),
        scratch_shapes=[pltpu.VMEM((tm, hidden), BF16)],
        compiler_params=_cparams(2),
        name="out_ffn",
    )(*operands, ma, o_f, o_b, proj, mc, md, modsel, hg, norm2.reshape(1, d), final_norm.reshape(1, d),
      bd, w_out, w_ffn_in, w_ffn_out)


def _rope_tables(tc, s, dim, width):
    n_freq = dim // 4
    inv_freq = ROPE_THETA ** (-jnp.arange(n_freq, dtype=F32) / n_freq)
    lane = np.arange(width) % dim
    is_col = (lane // (dim // 2)) == 1
    second = ((lane % (dim // 2)) // n_freq) == 1
    freq = inv_freq[lane % n_freq]
    ang_r = jnp.arange(s // GRID_W, dtype=F32)[:, None] * freq[None, :]
    ang_c = jnp.arange(GRID_W, dtype=F32)[:, None] * freq[None, :]

    def on_grid(f):
        return jnp.where(is_col[None, None, :], f(ang_c)[None, :, :], f(ang_r)[:, None, :]).reshape(s, width)

    cos, sin = on_grid(jnp.cos), on_grid(jnp.sin)
    sin_a = jnp.where(second[None, :], sin, 0.0)
    sin_b = jnp.where(second[None, :], 0.0, -sin)
    ident = jnp.ones((tc, width), F32)
    zero = jnp.zeros((tc, width), F32)
    return (jnp.concatenate([cos, ident], axis=0), jnp.concatenate([sin_a, zero], axis=0),
            jnp.concatenate([sin_b, zero], axis=0))


def kernel(x, c, ctx, c_ctx, w_mod, b_mod, norm1, w_in, diff_lambda, diff_norm, hgrn_lb_logits,
           hgrn_norm, q_norm, k_norm, w_out, norm2, w_ffn_in, w_ffn_out, final_norm):
    b, s, d = x.shape
    tc = ctx.shape[1]
    depth = w_mod.shape[0]
    assert tc == TILE and s % TILE == 0 and s % GRID_W == 0 and d == 4 * GROUP

    rope_a = _rope_tables(tc, s, A_QK, GROUP)
    rope_c = _rope_tables(tc, s, HEAD_DIM, GROUP)
    dft = _dft_tables(tc, s)
    hgrn_tabs = _hgrn_tables()
    hgrn_tabs = tuple(jnp.asarray(a, BF16 if i < 2 else F32) for i, a in enumerate(hgrn_tabs))
    bd = jnp.asarray(np.kron(np.eye(GROUP // HEAD_DIM), np.full((HEAD_DIM, HEAD_DIM), 1.0 / HEAD_DIM)), BF16)

    p = jax.nn.softmax(hgrn_lb_logits.astype(F32), axis=1)
    lower = jnp.cumsum(p, axis=1) - p[:, :1]

    def swap_heads(w, axis, start):
        cuts = [0, start + HEAD_DIM, start + 2 * HEAD_DIM, start + 3 * HEAD_DIM, w.shape[axis]]
        parts = [lax.slice_in_dim(w, cuts[i], cuts[i + 1], axis=axis) for i in range(4)]
        return jnp.concatenate([parts[0], parts[2], parts[1], parts[3]], axis=axis).astype(BF16)

    w_in_b = swap_heads(w_in, 2, COL_C_Q)
    w_out_b = swap_heads(w_out, 1, 2 * GROUP)
    w_ffn_in_b = w_ffn_in.astype(BF16)
    w_ffn_out_b = w_ffn_out.astype(BF16)

    rows = -(-(b + 1) // 8) * 8
    cc = jnp.zeros((rows, d), F32).at[:b].set(c).at[b].set(c_ctx)
    mod = _modulation(cc, w_mod, b_mod).reshape(depth, rows, 6, d)
    modsel = jnp.stack([jnp.broadcast_to(mod[:, b:b + 1], (depth, b, 6, d)), mod[:, :b]], axis=2)

    xs = (x, ctx)
    for l in range(depth):
        last = l == depth - 1
        lam_init = 0.8 - 0.6 * math.exp(-0.3 * l)
        proj = _in_proj(xs, s + tc, modsel[l], norm1[l], w_in_b, l)
        m_a, m_c = _attention(proj, rope_a, rope_c, diff_lambda[l], diff_norm[l], q_norm[l], k_norm[l],
                              bd, tc, lam_init)
        o_f, o_b = _hgrn(proj, lower[0, l], lower[1, l], hgrn_tabs)
        m_d = _fourier(proj, *dft, tc)
        if last and isinstance(xs, tuple):
            xs = xs[0]
        xs = _out_ffn(xs, m_a, o_f, o_b, proj, m_c, m_d, modsel[l], hgrn_norm[l], norm2[l], final_norm,
                      bd, w_out_b, w_ffn_in_b, w_ffn_out_b, l, s if last else s + tc, last)
    return xs
```
